```python
import jax, jax.numpy as jnp
from jax import lax
import numpy as np

D_MODEL = 1024
BATCH = 8
SEQ = 4096
DEPTH = 2

N_MIXERS = 2
N_RWKV_LAYERS = (DEPTH + 1) // 2
N_GMLP_LAYERS = DEPTH // 2
N_MEM = 256

RWKV_HEAD = 64
RWKV_HEADS = D_MODEL // RWKV_HEAD
LORA_DECAY = 64
LORA_AAA = 64
LORA_GATE = 160
GN_EPS = 64e-5

GMLP_CHUNK = 128
GMLP_WIDTH = 2 * D_MODEL
GMLP_GROUP_DIM = 128
GMLP_GROUPS = GMLP_WIDTH // GMLP_GROUP_DIM

XATTN_HEADS = 4
XATTN_HEAD_DIM = D_MODEL // XATTN_HEADS

D_FF = 2816
RMS_EPS = 1e-6

kernel_name = "rwkv7_gmlp_interleaved_macaron_memxattn"


def rms_norm(x, g):
    x32 = x.astype(jnp.float32)
    y = x32 * lax.rsqrt(jnp.mean(x32 * x32, axis=-1, keepdims=True) + RMS_EPS)
    return (y * g.astype(jnp.float32)).astype(x.dtype)


def swiglu_ffn(h, w_in, w_out):
    gate, up = jnp.split(h @ w_in, 2, axis=-1)
    return (jax.nn.silu(gate) * up) @ w_out


def token_shift(x):
    return jnp.pad(x[:, :-1], ((0, 0), (1, 0), (0, 0)))


def rwkv7_time_mix(h, mu, w_rkv, w0, w1, w2, a0, a1, a2, g1, g2, k_k, k_a, r_k, ln_w, ln_b, w_o):
    B, S, C = h.shape
    H, N = RWKV_HEADS, RWKV_HEAD
    dx = token_shift(h) - h
    xr = h + dx * mu[0]
    xw = h + dx * mu[1]
    xk = h + dx * mu[2]
    xv = h + dx * mu[3]
    xa = h + dx * mu[4]
    xg = h + dx * mu[5]
    x3 = jnp.stack([xr, xk, xv], axis=0)
    rkv = jnp.einsum('nbsc,cnd->nbsd', x3, w_rkv.reshape(C, 3, C))
    r, k, v = rkv[0], rkv[1], rkv[2]
    w = -jax.nn.softplus(-(w0 + jnp.tanh(xw @ w1) @ w2)) - 0.5
    decay = jnp.exp(-jnp.exp(w.astype(jnp.float32)))
    a = jax.nn.sigmoid(a0 + (xa @ a1) @ a2)
    g = jax.nn.sigmoid(xg @ g1) @ g2
    kk = (k * k_k).reshape(B, S, H, N).astype(jnp.float32)
    kk = kk / jnp.maximum(jnp.linalg.norm(kk, axis=-1, keepdims=True), 1e-12)
    k = k * (1.0 + (a - 1.0) * k_a)

    def heads(t):
        return t.reshape(B, S, H, N).astype(jnp.float32)

    r_h, k_h, v_h, a_h = heads(r), heads(k), heads(v), heads(a)
    b_h = kk * a_h
    xs = tuple(jnp.moveaxis(t, 1, 0) for t in (r_h, heads(decay), k_h, v_h, kk, b_h))

    def step(state, inp):
        r_t, w_t, k_t, v_t, kk_t, b_t = inp
        sa = jnp.einsum('bhij,bhj->bhi', state, -kk_t)
        state = (state * w_t[:, :, None, :] + sa[..., None] * b_t[:, :, None, :]
                 + v_t[..., :, None] * k_t[:, :, None, :])
        y = jnp.einsum('bhij,bhj->bhi', state, r_t)
        return state, y

    s0 = jnp.zeros((B, H, N, N), jnp.float32)
    _, ys = lax.scan(step, s0, xs)
    y = jnp.moveaxis(ys, 0, 1)
    mean = jnp.mean(y, axis=-1, keepdims=True)
    var = jnp.mean(jnp.square(y - mean), axis=-1, keepdims=True)
    y = ((y - mean) * lax.rsqrt(var + GN_EPS)).reshape(B, S, C)
    y = y * ln_w.astype(jnp.float32) + ln_b.astype(jnp.float32)
    bonus = jnp.sum(r_h * k_h * r_k.astype(jnp.float32), axis=-1, keepdims=True) * v_h
    y = (y + bonus.reshape(B, S, C)).astype(h.dtype)
    return (y * g) @ w_o


def gmlp_chunk_mix(h, w_uv, v_norm, w_s, b_s, w_o):
    B, S, _ = h.shape
    z = jax.nn.gelu(h @ w_uv)
    u, v = jnp.split(z, 2, axis=-1)
    v = rms_norm(v, v_norm)
    v = v.reshape(B, S // GMLP_CHUNK, GMLP_CHUNK, GMLP_GROUPS, GMLP_GROUP_DIM)
    causal = jnp.tril(jnp.ones((GMLP_CHUNK, GMLP_CHUNK), dtype=bool))
    ws = jnp.where(causal[None], w_s, jnp.zeros_like(w_s))
    sv = jnp.einsum('gts,bcsgd->bctgd', ws, v) + jnp.transpose(b_s)[:, :, None]
    sv = sv.reshape(B, S, GMLP_WIDTH)
    return (u * sv) @ w_o


def memory_cross_attention(h, mem_k, mem_v, wq, wo):
    B, S, C = h.shape
    q = (h @ wq).reshape(B, S, XATTN_HEADS, XATTN_HEAD_DIM)
    s = jnp.einsum('bshd,bmhd->bhsm', q, mem_k).astype(jnp.float32) * (XATTN_HEAD_DIM ** -0.5)
    p = jax.nn.softmax(s, axis=-1).astype(h.dtype)
    o = jnp.einsum('bhsm,bmhd->bshd', p, mem_v).reshape(B, S, C)
    return o @ wo


def setup_inputs(seed: int = 0) -> dict:
    key = jax.random.key(seed)
    ks = iter(jax.random.split(key, 48))
    C, F = D_MODEL, D_FF
    L, NA, NB = DEPTH, N_RWKV_LAYERS, N_GMLP_LAYERS

    def nrm(shape, scale):
        return jax.random.normal(next(ks), shape, jnp.float32) * scale

    def gain(shape):
        return 1.0 + 0.05 * jax.random.normal(next(ks), shape, jnp.float32)

    return {
        "x": jax.random.normal(next(ks), (BATCH, SEQ, C), jnp.float32),
        "mem": jax.random.normal(next(ks), (BATCH, N_MEM, C), jnp.float32),
        "mem_norm": gain((C,)),
        "mem_w_kv": nrm((C, 2 * C), C ** -0.5),
        "ffn1_norm": gain((L, C)),
        "ffn1_w_in": nrm((L, C, 2 * F), C ** -0.5),
        "ffn1_w_out": nrm((L, F, C), F ** -0.5),
        "mix_norm": gain((L, C)),
        "xattn_norm": gain((L, C)),
        "xattn_wq": nrm((L, C, C), C ** -0.5),
        "xattn_wo": nrm((L, C, C), C ** -0.5),
        "ffn2_norm": gain((L, C)),
        "ffn2_w_in": nrm((L, C, 2 * F), C ** -0.5),
        "ffn2_w_out": nrm((L, F, C), F ** -0.5),
        "rwkv_mu": jax.random.uniform(next(ks), (NA, 6, C), jnp.float32),
        "rwkv_w_rkv": nrm((NA, C, 3 * C), C ** -0.5),
        "rwkv_w0": jax.random.uniform(next(ks), (NA, C), jnp.float32, -6.0, -0.5),
        "rwkv_w1": nrm((NA, C, LORA_DECAY), C ** -0.5),
        "rwkv_w2": nrm((NA, LORA_DECAY, C), 0.5 * LORA_DECAY ** -0.5),
        "rwkv_a0": nrm((NA, C), 0.1),
        "rwkv_a1": nrm((NA, C, LORA_AAA), C ** -0.5),
        "rwkv_a2": nrm((NA, LORA_AAA, C), 0.5 * LORA_AAA ** -0.5),
        "rwkv_g1": nrm((NA, C, LORA_GATE), C ** -0.5),
        "rwkv_g2": nrm((NA, LORA_GATE, C), LORA_GATE ** -0.5),
        "rwkv_k_k": gain((NA, C)),
        "rwkv_k_a": gain((NA, C)),
        "rwkv_r_k": nrm((NA, RWKV_HEADS, RWKV_HEAD), 0.1),
        "rwkv_ln_w": gain((NA, C)),
        "rwkv_ln_b": nrm((NA, C), 0.01),
        "rwkv_w_o": nrm((NA, C, C), C ** -0.5),
        "gmlp_w_uv": nrm((NB, C, 2 * GMLP_WIDTH), C ** -0.5),
        "gmlp_v_norm": gain((NB, GMLP_WIDTH)),
        "gmlp_w_s": nrm((NB, GMLP_GROUPS, GMLP_CHUNK, GMLP_CHUNK), 0.5 * GMLP_CHUNK ** -0.5),
        "gmlp_b_s": gain((NB, GMLP_GROUPS, GMLP_CHUNK)),
        "gmlp_w_o": nrm((NB, GMLP_WIDTH, C), GMLP_WIDTH ** -0.5),
        "final_norm": gain((C,)),
    }


def reference(x, mem, mem_norm, mem_w_kv,
              ffn1_norm, ffn1_w_in, ffn1_w_out, mix_norm, xattn_norm, xattn_wq, xattn_wo,
              ffn2_norm, ffn2_w_in, ffn2_w_out,
              rwkv_mu, rwkv_w_rkv, rwkv_w0, rwkv_w1, rwkv_w2, rwkv_a0, rwkv_a1, rwkv_a2,
              rwkv_g1, rwkv_g2, rwkv_k_k, rwkv_k_a, rwkv_r_k, rwkv_ln_w, rwkv_ln_b, rwkv_w_o,
              gmlp_w_uv, gmlp_v_norm, gmlp_w_s, gmlp_b_s, gmlp_w_o,
              final_norm):
    B = mem.shape[0]
    mem_k, mem_v = jnp.split(rms_norm(mem, mem_norm) @ mem_w_kv, 2, axis=-1)
    mem_k = mem_k.reshape(B, N_MEM, XATTN_HEADS, XATTN_HEAD_DIM)
    mem_v = mem_v.reshape(B, N_MEM, XATTN_HEADS, XATTN_HEAD_DIM)

    for i in range(DEPTH):
        x = x + 0.5 * swiglu_ffn(rms_norm(x, ffn1_norm[i]), ffn1_w_in[i], ffn1_w_out[i])
        h = rms_norm(x, mix_norm[i])
        j = i // N_MIXERS
        if i % N_MIXERS == 0:
            y = rwkv7_time_mix(h, rwkv_mu[j], rwkv_w_rkv[j], rwkv_w0[j], rwkv_w1[j], rwkv_w2[j],
                               rwkv_a0[j], rwkv_a1[j], rwkv_a2[j], rwkv_g1[j], rwkv_g2[j],
                               rwkv_k_k[j], rwkv_k_a[j], rwkv_r_k[j], rwkv_ln_w[j], rwkv_ln_b[j],
                               rwkv_w_o[j])
        else:
            y = gmlp_chunk_mix(h, gmlp_w_uv[j], gmlp_v_norm[j], gmlp_w_s[j], gmlp_b_s[j], gmlp_w_o[j])
        x = x + y
        x = x + memory_cross_attention(rms_norm(x, xattn_norm[i]), mem_k, mem_v,
                                       xattn_wq[i], xattn_wo[i])
        x = x + 0.5 * swiglu_ffn(rms_norm(x, ffn2_norm[i]), ffn2_w_in[i], ffn2_w_out[i])
    return rms_norm(x, final_norm)
```

```python
import functools

import jax
import jax.numpy as jnp
from jax import lax
from jax.experimental import pallas as pl
from jax.experimental.pallas import tpu as pltpu

F32 = jnp.float32
BF16 = jnp.bfloat16

RMS_EPS = 1e-6
GN_EPS = 64e-5
RWKV_HEAD = 64
GMLP_CHUNK = 128
GMLP_GROUP_DIM = 128
XATTN_HEADS = 4

LANES = 128
SCAN_CHUNK = 64
SCAN_LANES = 256
VMEM_LIMIT = 52 * 1024 * 1024


def _resident(shape):
    nd = len(shape)
    return pl.BlockSpec(shape, lambda *_: (0,) * nd, pipeline_mode=pl.Buffered(1))


def _params(*sem):
    return pltpu.CompilerParams(dimension_semantics=sem, vmem_limit_bytes=VMEM_LIMIT)


def _rms(x, g):
    return x * lax.rsqrt(jnp.mean(x * x, axis=-1, keepdims=True) + RMS_EPS) * g


def _dot(a, b):
    return jnp.dot(a.astype(BF16), b.astype(BF16), preferred_element_type=F32)


def _dot_nt(a, b):
    return lax.dot_general(a.astype(BF16), b.astype(BF16), (((1,), (1,)), ((), ())),
                           preferred_element_type=F32)


def _dot_tn(a, b):
    return lax.dot_general(a.astype(BF16), b.astype(BF16), (((0,), (0,)), ((), ())),
                           preferred_element_type=F32)


def _split2(x):
    hi = x.astype(BF16)
    lo = (x - hi.astype(F32)).astype(BF16)
    return hi, lo


def _dot_exact_rhs(x, m):
    hi, lo = _split2(x)
    return (jnp.dot(hi, m, preferred_element_type=F32)
            + jnp.dot(lo, m, preferred_element_type=F32))


def _mem_kv_kernel(m_ref, g_ref, w_ref, k_ref, v_ref):
    c = m_ref.shape[-1]
    h = _rms(m_ref[...], g_ref[...])
    kv = _dot(h, w_ref[...])
    k_ref[...] = kv[:, :c].astype(BF16)
    v_ref[...] = kv[:, c:].astype(BF16)


def _mem_kv(mem2d, g, w_kv):
    t, c = mem2d.shape
    tm = 512
    out = jax.ShapeDtypeStruct((t, c), BF16)
    return pl.pallas_call(
        _mem_kv_kernel,
        grid=(t // tm,),
        in_specs=[pl.BlockSpec((tm, c), lambda i: (i, 0)),
                  _resident((1, c)), _resident((c, 2 * c))],
        out_specs=[pl.BlockSpec((tm, c), lambda i: (i, 0))] * 2,
        out_shape=[out, out],
        compiler_params=_params("parallel"),
        name="mem_kv",
    )(mem2d, g, w_kv)


def _ffn_kernel(x_ref, g_ref, win_ref, wout_ref, *rest, f_chunk, final):
    o_ref = rest[-1]
    d_ff = wout_ref.shape[0]
    x = x_ref[...]
    h = _rms(x, g_ref[...]).astype(BF16)
    acc = jnp.zeros_like(x)
    for lo in range(0, d_ff, f_chunk):
        gate = jnp.dot(h, win_ref[:, lo:lo + f_chunk], preferred_element_type=F32)
        up = jnp.dot(h, win_ref[:, d_ff + lo:d_ff + lo + f_chunk], preferred_element_type=F32)
        act = (gate * jax.nn.sigmoid(gate) * up).astype(BF16)
        acc = acc + jnp.dot(act, wout_ref[lo:lo + f_chunk, :], preferred_element_type=F32)
    y = x + 0.5 * acc
    if final:
        y = _rms(y, rest[0][...])
    o_ref[...] = y


def _ffn(x2d, g, w_in, w_out, final_g=None):
    t, c = x2d.shape
    d_ff = w_out.shape[0]
    tm = 512
    f_chunk = d_ff // 2
    final = final_g is not None
    in_specs = [pl.BlockSpec((tm, c), lambda i: (i, 0)),
                _resident((1, c)), _resident((c, 2 * d_ff)), _resident((d_ff, c))]
    args = [x2d, g, w_in, w_out]
    if final:
        in_specs.append(_resident((1, c)))
        args.append(final_g)
    return pl.pallas_call(
        functools.partial(_ffn_kernel, f_chunk=f_chunk, final=final),
        grid=(t // tm,),
        in_specs=in_specs,
        out_specs=pl.BlockSpec((tm, c), lambda i: (i, 0)),
        out_shape=jax.ShapeDtypeStruct((t, c), F32),
        compiler_params=_params("parallel"),
        name="ffn_final" if final else "ffn",
    )(*args)


def _xattn_kernel(x_ref, g_ref, wq_ref, wo_ref, k_ref, v_ref, o_ref, att_ref):
    x = x_ref[0]
    c = x.shape[-1]
    hd = c // XATTN_HEADS
    h = _rms(x, g_ref[...])
    q = (_dot(h, wq_ref[...]) * (hd ** -0.5)).astype(BF16)
    for i in range(XATTN_HEADS):
        sl = slice(i * hd, (i + 1) * hd)
        s = _dot_nt(q[:, sl], k_ref[0, :, sl])
        p = jnp.exp(s - jnp.max(s, axis=-1, keepdims=True))
        l = jnp.sum(p, axis=-1, keepdims=True)
        att_ref[:, sl] = (_dot(p, v_ref[0, :, sl]) / l).astype(BF16)
    o_ref[0] = x + jnp.dot(att_ref[...], wo_ref[...], preferred_element_type=F32)


def _xattn(x3d, g, wq, wo, mem_k, mem_v):
    b, s, c = x3d.shape
    m = mem_k.shape[1]
    tm = 512
    return pl.pallas_call(
        _xattn_kernel,
        grid=(b, s // tm),
        in_specs=[pl.BlockSpec((1, tm, c), lambda i, j: (i, j, 0)),
                  _resident((1, c)), _resident((c, c)), _resident((c, c)),
                  pl.BlockSpec((1, m, c), lambda i, j: (i, 0, 0)),
                  pl.BlockSpec((1, m, c), lambda i, j: (i, 0, 0))],
        out_specs=pl.BlockSpec((1, tm, c), lambda i, j: (i, j, 0)),
        out_shape=jax.ShapeDtypeStruct((b, s, c), F32),
        scratch_shapes=[pltpu.VMEM((tm, c), BF16)],
        compiler_params=_params("parallel", "parallel"),
        name="xattn",
    )(x3d, g, wq, wo, mem_k, mem_v)


def _rwkv_proj_kernel(x_ref, g_ref, mu_ref, wr_ref, wk_ref, wv_ref, w0_ref, w1_ref, w2_ref,
                      a0_ref, a1_ref, a2_ref, g1_ref, g2_ref, kk_ref, ka_ref, seg_ref, exp_ref,
                      r_out, lw_out, k_out, v_out, kk_out, b_out, g_out, carry_ref):
    @pl.when(pl.program_id(1) == 0)
    def _():
        carry_ref[...] = jnp.zeros_like(carry_ref)

    x = x_ref[0]
    tm = x.shape[0]
    h = _rms(x, g_ref[...])
    row = lax.broadcasted_iota(jnp.int32, h.shape, 0)
    prev = jnp.where(row == 0, carry_ref[0:1, :], pltpu.roll(h, 1, axis=0))
    carry_ref[0:1, :] = h[tm - 1:tm, :]
    dx = prev - h

    def mix(i):
        return h + dx * mu_ref[i:i + 1, :]

    r = _dot(mix(0), wr_ref[...])
    k = _dot(mix(2), wk_ref[...])
    v = _dot(mix(3), wv_ref[...])
    z = w0_ref[...] + _dot(jnp.tanh(_dot(mix(1), w1_ref[...])), w2_ref[...])
    u = -z
    softplus = jnp.maximum(u, 0.0) + jnp.log(1.0 + jnp.exp(-jnp.abs(u)))
    lw_out[0] = -jnp.exp(-softplus - 0.5)
    a = jax.nn.sigmoid(a0_ref[...] + _dot(_dot(mix(4), a1_ref[...]), a2_ref[...]))
    g_out[0] = _dot(jax.nn.sigmoid(_dot(mix(5), g1_ref[...])), g2_ref[...])

    kkr = k * kk_ref[...]
    nrm = jnp.maximum(jnp.sqrt(_dot_exact_rhs(kkr * kkr, seg_ref[...])), 1e-12)
    kk = kkr * _dot_exact_rhs(1.0 / nrm, exp_ref[...])
    r_out[0] = r
    v_out[0] = v
    k_out[0] = k * (1.0 + (a - 1.0) * ka_ref[...])
    kk_out[0] = kk
    b_out[0] = kk * a


def _rwkv_proj(x3d, g, mu, wr, wk, wv, w0, w1, w2, a0, a1, a2, g1, g2, k_k, k_a, seg, exp):
    b, s, c = x3d.shape
    tm = 256
    tok = pl.BlockSpec((1, tm, c), lambda i, j: (i, j, 0))
    consts = [g, mu, wr, wk, wv, w0, w1, w2, a0, a1, a2, g1, g2, k_k, k_a, seg, exp]
    out = jax.ShapeDtypeStruct((b, s, c), F32)
    return pl.pallas_call(
        _rwkv_proj_kernel,
        grid=(b, s // tm),
        in_specs=[tok] + [_resident(a.shape) for a in consts],
        out_specs=[tok] * 7,
        out_shape=[out] * 7,
        scratch_shapes=[pltpu.VMEM((8, c), F32)],
        compiler_params=_params("parallel", "arbitrary"),
        name="rwkv_proj",
    )(x3d, *consts)


def _scan_kernel(r_ref, lw_ref, k_ref, v_ref, kk_ref, b_ref, y_ref, s_ref):
    L = SCAN_CHUNK
    W = SCAN_LANES
    G = W // RWKV_HEAD

    @pl.when(pl.program_id(2) == 0)
    def _():
        s_ref[...] = jnp.zeros_like(s_ref)

    rr = lax.broadcasted_iota(jnp.int32, (W, W), 0)
    cc = lax.broadcasted_iota(jnp.int32, (W, W), 1)
    bd = (rr // RWKV_HEAD == cc // RWKV_HEAD).astype(F32)
    tr = lax.broadcasted_iota(jnp.int32, (L, W), 0)
    tc = lax.broadcasted_iota(jnp.int32, (L, W), 1) % L
    tril_incl = (tc <= tr).astype(F32)
    tril_strict = (tc < tr).astype(F32)
    eye = (tc == tr).astype(F32)
    t64r = lax.broadcasted_iota(jnp.int32, (L, L), 0)
    t64c = lax.broadcasted_iota(jnp.int32, (L, L), 1)
    tri = (t64c <= t64r).astype(BF16)

    def stack(x):
        return jnp.concatenate([x] * G, axis=0) * bd

    def chunk(i, carry):
        sl = pl.ds(pl.multiple_of(i * L, L), L)
        r = r_ref[0, sl, :]
        lw = lw_ref[0, sl, :]
        k = k_ref[0, sl, :]
        v = v_ref[0, sl, :]
        kk = kk_ref[0, sl, :]
        b = b_ref[0, sl, :]
        state = s_ref[...]

        hi = lw.astype(BF16)
        rem = lw - hi.astype(F32)
        mid = rem.astype(BF16)
        lo = (rem - mid.astype(F32)).astype(BF16)
        cum = (jnp.dot(tri, hi, preferred_element_type=F32)
               + jnp.dot(tri, mid, preferred_element_type=F32)
               + jnp.dot(tri, lo, preferred_element_type=F32))
        gam = jnp.exp(cum)
        inv_gam = jnp.exp(-cum)
        cum_last = cum[L - 1:L, :]
        to_end = jnp.exp(cum_last - cum)

        x = jnp.concatenate([r * gam, kk * jnp.exp(cum - lw)], axis=0)
        ysm = jnp.concatenate([stack(b * inv_gam), stack(k * inv_gam)], axis=0)
        xs = _dot_nt(x, state)
        acs = _dot_nt(x, ysm)
        a_rb = acs[:L, :W] * tril_incl
        a_rk = acs[:L, W:] * tril_incl
        a_ab = acs[L:, :W] * tril_strict
        a_ak = acs[L:, W:] * tril_strict
        vsm = stack(v)
        w = xs[L:] + _dot(a_ak, vsm)

        m = -a_ab
        p = eye + m
        mbd = stack(m)
        for _ in range(5):
            m = _dot(m, mbd)
            mbd = stack(m)
            p = p + _dot(p, mbd)

        sa = -_dot(p, stack(w))
        y = xs[:L] + _dot(jnp.concatenate([a_rb, a_rk], axis=1),
                          jnp.concatenate([stack(sa), vsm], axis=0))
        y_ref[0, sl, :] = y
        upd = _dot_tn(jnp.concatenate([sa, v], axis=0),
                      jnp.concatenate([b * to_end, k * to_end], axis=0))
        s_ref[...] = state * jnp.exp(cum_last) + upd * bd
        return carry

    lax.fori_loop(0, r_ref.shape[1] // L, chunk, 0)


def _rwkv_scan(r, lw, k, v, kk, b):
    bsz, s, c = r.shape
    lt = 256
    tok = pl.BlockSpec((1, lt, SCAN_LANES), lambda i, h, j: (i, j, h))
    return pl.pallas_call(
        _scan_kernel,
        grid=(bsz, c // SCAN_LANES, s // lt),
        in_specs=[tok] * 6,
        out_specs=tok,
        out_shape=jax.ShapeDtypeStruct((bsz, s, c), F32),
        scratch_shapes=[pltpu.VMEM((SCAN_LANES, SCAN_LANES), F32)],
        compiler_params=_params("parallel", "parallel", "arbitrary"),
        name="rwkv_scan",
    )(r, lw, k, v, kk, b)


def _rwkv_out_kernel(x_ref, y_ref, r_ref, k_ref, v_ref, g_ref, rk_ref, lnw_ref, lnb_ref,
                     wo_ref, seg_ref, exp_ref, o_ref):
    seg = seg_ref[...]
    exp = exp_ref[...]

    def head_sum(t):
        return _dot_exact_rhs(_dot_exact_rhs(t, seg), exp)

    y = y_ref[...]
    inv_n = 1.0 / RWKV_HEAD
    d = y - head_sum(y) * inv_n
    var = head_sum(d * d) * inv_n
    yn = d * lax.rsqrt(var + GN_EPS) * lnw_ref[...] + lnb_ref[...]
    bonus = head_sum(r_ref[...] * k_ref[...] * rk_ref[...]) * v_ref[...]
    o_ref[...] = x_ref[...] + _dot((yn + bonus) * g_ref[...], wo_ref[...])


def _rwkv_out(x2d, y, r, k, v, g, r_k, ln_w, ln_b, w_o, seg, exp):
    t, c = x2d.shape
    tm = 512
    tok = pl.BlockSpec((tm, c), lambda i: (i, 0))
    consts = [r_k, ln_w, ln_b, w_o, seg, exp]
    return pl.pallas_call(
        _rwkv_out_kernel,
        grid=(t // tm,),
        in_specs=[tok] * 6 + [_resident(a.shape) for a in consts],
        out_specs=tok,
        out_shape=jax.ShapeDtypeStruct((t, c), F32),
        compiler_params=_params("parallel"),
        name="rwkv_out",
    )(x2d, y, r, k, v, g, *consts)


def _gelu_tanh(x):
    return 0.5 * x * (1.0 + jnp.tanh(0.7978845608028654 * (x + 0.044715 * (x * x * x))))


def _gmlp_kernel(x_ref, g_ref, wuv_ref, vn_ref, ws_ref, bs_ref, wo_ref, o_ref, gated_ref):
    x = x_ref[...]
    tm = x.shape[0]
    width = wo_ref.shape[0]
    T = GMLP_CHUNK
    D = GMLP_GROUP_DIM
    h = _rms(x, g_ref[...]).astype(BF16)
    u = _gelu_tanh(jnp.dot(h, wuv_ref[:, :width], preferred_element_type=F32))
    v = _gelu_tanh(jnp.dot(h, wuv_ref[:, width:], preferred_element_type=F32))
    v = _rms(v, vn_ref[...]).astype(BF16)
    row = lax.broadcasted_iota(jnp.int32, (T, T), 0)
    col = lax.broadcasted_iota(jnp.int32, (T, T), 1)
    causal = col <= row
    for grp in range(width // D):
        lanes = slice(grp * D, (grp + 1) * D)
        ws = jnp.where(causal, ws_ref[grp], 0.0).astype(BF16)
        bias = bs_ref[:, grp:grp + 1]
        for ch in range(tm // T):
            rows = slice(ch * T, (ch + 1) * T)
            sv = jnp.dot(ws, v[rows, lanes], preferred_element_type=F32) + bias
            gated_ref[rows, lanes] = (u[rows, lanes] * sv).astype(BF16)
    o_ref[...] = x + jnp.dot(gated_ref[...], wo_ref[...], preferred_element_type=F32)


def _gmlp(x2d, g, w_uv, v_norm, w_s, b_s_t, w_o):
    t, c = x2d.shape
    width = w_o.shape[0]
    tm = 256
    consts = [g, w_uv, v_norm, w_s, b_s_t, w_o]
    return pl.pallas_call(
        _gmlp_kernel,
        grid=(t // tm,),
        in_specs=[pl.BlockSpec((tm, c), lambda i: (i, 0))] + [_resident(a.shape) for a in consts],
        out_specs=pl.BlockSpec((tm, c), lambda i: (i, 0)),
        out_shape=jax.ShapeDtypeStruct((t, c), F32),
        scratch_shapes=[pltpu.VMEM((tm, width), BF16)],
        compiler_params=_params("parallel"),
        name="gmlp",
    )(x2d, *consts)


def _pad_cols(w, n):
    return jnp.pad(w, ((0, 0), (0, n - w.shape[1])))


def _pad_rows(w, n):
    return jnp.pad(w, ((0, n - w.shape[0]), (0, 0)))


def _round_up(n, m):
    return -(-n // m) * m


def _rwkv_layer(x3d, mix_g, mu, w_rkv, w0, w1, w2, a0, a1, a2, g1, g2, k_k, k_a, r_k,
                ln_w, ln_b, w_o):
    b, s, c = x3d.shape
    heads = c // RWKV_HEAD
    head_of = jnp.arange(c) // RWKV_HEAD
    seg = (head_of[:, None] == jnp.arange(LANES)[None, :]).astype(BF16)
    exp = seg.T
    row = lambda p: p.reshape(1, c)
    rd = _round_up(w1.shape[1], LANES)
    ra = _round_up(a1.shape[1], LANES)
    rg = _round_up(g1.shape[1], LANES)
    wb = w_rkv.astype(BF16)
    r, lw, k, v, kk, bb, g = _rwkv_proj(
        x3d, row(mix_g), mu, wb[:, :c], wb[:, c:2 * c], wb[:, 2 * c:], row(w0),
        _pad_cols(w1, rd).astype(BF16), _pad_rows(w2, rd).astype(BF16), row(a0),
        _pad_cols(a1, ra).astype(BF16), _pad_rows(a2, ra).astype(BF16),
        _pad_cols(g1, rg).astype(BF16), _pad_rows(g2, rg).astype(BF16),
        row(k_k), row(k_a), seg, exp)
    y = _rwkv_scan(r, lw, k, v, kk, bb)
    flat = lambda t: t.reshape(b * s, c)
    out = _rwkv_out(flat(x3d), flat(y), flat(r), flat(k), flat(v), flat(g),
                    r_k.reshape(1, heads * RWKV_HEAD), row(ln_w), row(ln_b),
                    w_o.astype(BF16), seg, exp)
    return out.reshape(b, s, c)


def kernel(x, mem, mem_norm, mem_w_kv, ffn1_norm, ffn1_w_in, ffn1_w_out, mix_norm, xattn_norm, xattn_wq, xattn_wo, ffn2_norm, ffn2_w_in, ffn2_w_out, rwkv_mu, rwkv_w_rkv, rwkv_w0, rwkv_w1, rwkv_w2, rwkv_a0, rwkv_a1, rwkv_a2, rwkv_g1, rwkv_g2, rwkv_k_k, rwkv_k_a, rwkv_r_k, rwkv_ln_w, rwkv_ln_b, rwkv_w_o, gmlp_w_uv, gmlp_v_norm, gmlp_w_s, gmlp_b_s, gmlp_w_o, final_norm):
    b, s, c = x.shape
    m = mem.shape[1]
    depth = ffn1_norm.shape[0]
    row = lambda p: p.reshape(1, -1)

    mem_k, mem_v = _mem_kv(mem.reshape(b * m, c), row(mem_norm), mem_w_kv.astype(BF16))
    mem_k = mem_k.reshape(b, m, c)
    mem_v = mem_v.reshape(b, m, c)

    x2d = x.reshape(b * s, c)
    for i in range(depth):
        x2d = _ffn(x2d, row(ffn1_norm[i]), ffn1_w_in[i].astype(BF16), ffn1_w_out[i].astype(BF16))
        j = i // 2
        if i % 2 == 0:
            x3d = _rwkv_layer(
                x2d.reshape(b, s, c), mix_norm[i], rwkv_mu[j], rwkv_w_rkv[j], rwkv_w0[j],
                rwkv_w1[j], rwkv_w2[j], rwkv_a0[j], rwkv_a1[j], rwkv_a2[j], rwkv_g1[j],
                rwkv_g2[j], rwkv_k_k[j], rwkv_k_a[j], rwkv_r_k[j], rwkv_ln_w[j],
                rwkv_ln_b[j], rwkv_w_o[j])
        else:
            x2d = _gmlp(x2d, row(mix_norm[i]), gmlp_w_uv[j].astype(BF16), row(gmlp_v_norm[j]),
                        gmlp_w_s[j], jnp.transpose(gmlp_b_s[j]), gmlp_w_o[j].astype(BF16))
            x3d = x2d.reshape(b, s, c)
        x3d = _xattn(x3d, row(xattn_norm[i]), xattn_wq[i].astype(BF16),
                     xattn_wo[i].astype(BF16), mem_k, mem_v)
        x2d = _ffn(x3d.reshape(b * s, c), row(ffn2_norm[i]), ffn2_w_in[i].astype(BF16),
                   ffn2_w_out[i].astype(BF16),
                   final_g=row(final_norm) if i == depth - 1 else None)
    return x2d.reshape(b, s, c)
```

```python
import functools

import jax
import jax.numpy as jnp
from jax import lax
from jax.experimental import pallas as pl
from jax.experimental.pallas import tpu as pltpu

F32 = jnp.float32
BF16 = jnp.bfloat16

RMS_EPS = 1e-6
GN_EPS = 64e-5
RWKV_HEAD = 64
GMLP_CHUNK = 128
GMLP_GROUP_DIM = 128
XATTN_HEADS = 4

LANES = 128
SCAN_CHUNK = 64
SCAN_LANES = 256
VMEM_LIMIT = 52 * 1024 * 1024


def _resident(shape):
    nd = len(shape)
    return pl.BlockSpec(shape, lambda *_: (0,) * nd, pipeline_mode=pl.Buffered(1))


def _params(*sem):
    return pltpu.CompilerParams(dimension_semantics=sem, vmem_limit_bytes=VMEM_LIMIT)


def _rms(x, g):
    return x * lax.rsqrt(jnp.mean(x * x, axis=-1, keepdims=True) + RMS_EPS) * g


def _dot(a, b):
    return jnp.dot(a.astype(BF16), b.astype(BF16), preferred_element_type=F32)


def _dot_nt(a, b):
    return lax.dot_general(a.astype(BF16), b.astype(BF16), (((1,), (1,)), ((), ())),
                           preferred_element_type=F32)


def _dot_tn(a, b):
    return lax.dot_general(a.astype(BF16), b.astype(BF16), (((0,), (0,)), ((), ())),
                           preferred_element_type=F32)


def _split2(x):
    hi = x.astype(BF16)
    lo = (x - hi.astype(F32)).astype(BF16)
    return hi, lo


def _dot_exact_rhs(x, m):
    hi, lo = _split2(x)
    return (jnp.dot(hi, m, preferred_element_type=F32)
            + jnp.dot(lo, m, preferred_element_type=F32))


def _mem_kv_kernel(m_ref, g_ref, w_ref, k_ref, v_ref):
    c = m_ref.shape[-1]
    h = _rms(m_ref[...], g_ref[...])
    kv = _dot(h, w_ref[...])
    k_ref[...] = kv[:, :c].astype(BF16)
    v_ref[...] = kv[:, c:].astype(BF16)


def _mem_kv(mem2d, g, w_kv):
    t, c = mem2d.shape
    tm = 512
    out = jax.ShapeDtypeStruct((t, c), BF16)
    return pl.pallas_call(
        _mem_kv_kernel,
        grid=(t // tm,),
        in_specs=[pl.BlockSpec((tm, c), lambda i: (i, 0)),
                  _resident((1, c)), _resident((c, 2 * c))],
        out_specs=[pl.BlockSpec((tm, c), lambda i: (i, 0))] * 2,
        out_shape=[out, out],
        compiler_params=_params("parallel"),
        name="mem_kv",
    )(mem2d, g, w_kv)


def _ffn_kernel(x_ref, g_ref, win_ref, wout_ref, *rest, f_chunk, final):
    o_ref = rest[-1]
    d_ff = wout_ref.shape[0]
    x = x_ref[...]
    h = _rms(x, g_ref[...]).astype(BF16)
    acc = jnp.zeros_like(x)
    for lo in range(0, d_ff, f_chunk):
        gate = jnp.dot(h, win_ref[:, lo:lo + f_chunk], preferred_element_type=F32)
        up = jnp.dot(h, win_ref[:, d_ff + lo:d_ff + lo + f_chunk], preferred_element_type=F32)
        act = (gate * jax.nn.sigmoid(gate) * up).astype(BF16)
        acc = acc + jnp.dot(act, wout_ref[lo:lo + f_chunk, :], preferred_element_type=F32)
    y = x + 0.5 * acc
    if final:
        y = _rms(y, rest[0][...])
    o_ref[...] = y


def _ffn(x2d, g, w_in, w_out, final_g=None):
    t, c = x2d.shape
    d_ff = w_out.shape[0]
    tm = 512
    f_chunk = d_ff // 2
    final = final_g is not None
    in_specs = [pl.BlockSpec((tm, c), lambda i: (i, 0)),
                _resident((1, c)), _resident((c, 2 * d_ff)), _resident((d_ff, c))]
    args = [x2d, g, w_in, w_out]
    if final:
        in_specs.append(_resident((1, c)))
        args.append(final_g)
    return pl.pallas_call(
        functools.partial(_ffn_kernel, f_chunk=f_chunk, final=final),
        grid=(t // tm,),
        in_specs=in_specs,
        out_specs=pl.BlockSpec((tm, c), lambda i: (i, 0)),
        out_shape=jax.ShapeDtypeStruct((t, c), F32),
        compiler_params=_params("parallel"),
        name="ffn_final" if final else "ffn",
    )(*args)


def _xattn_kernel(x_ref, g_ref, wq_ref, wo_ref, k_ref, v_ref, o_ref, att_ref):
    x = x_ref[0]
    c = x.shape[-1]
    hd = c // XATTN_HEADS
    h = _rms(x, g_ref[...])
    q = (_dot(h, wq_ref[...]) * (hd ** -0.5)).astype(BF16)
    for i in range(XATTN_HEADS):
        sl = slice(i * hd, (i + 1) * hd)
        s = _dot_nt(q[:, sl], k_ref[0, :, sl])
        p = jnp.exp(s - jnp.max(s, axis=-1, keepdims=True))
        l = jnp.sum(p, axis=-1, keepdims=True)
        att_ref[:, sl] = (_dot(p, v_ref[0, :, sl]) / l).astype(BF16)
    o_ref[0] = x + jnp.dot(att_ref[...], wo_ref[...], preferred_element_type=F32)


def _xattn(x3d, g, wq, wo, mem_k, mem_v):
    b, s, c = x3d.shape
    m = mem_k.shape[1]
    tm = 512
    return pl.pallas_call(
        _xattn_kernel,
        grid=(b, s // tm),
        in_specs=[pl.BlockSpec((1, tm, c), lambda i, j: (i, j, 0)),
                  _resident((1, c)), _resident((c, c)), _resident((c, c)),
                  pl.BlockSpec((1, m, c), lambda i, j: (i, 0, 0)),
                  pl.BlockSpec((1, m, c), lambda i, j: (i, 0, 0))],
        out_specs=pl.BlockSpec((1, tm, c), lambda i, j: (i, j, 0)),
        out_shape=jax.ShapeDtypeStruct((b, s, c), F32),
        scratch_shapes=[pltpu.VMEM((tm, c), BF16)],
        compiler_params=_params("parallel", "parallel"),
        name="xattn",
    )(x3d, g, wq, wo, mem_k, mem_v)


def _rwkv_proj_kernel(x_ref, g_ref, mu_ref, wr_ref, wk_ref, wv_ref, w0_ref, w1_ref, w2_ref,
                      a0_ref, a1_ref, a2_ref, g1_ref, g2_ref, kk_ref, ka_ref, seg_ref, exp_ref,
                      r_out, lw_out, k_out, v_out, kk_out, b_out, g_out, carry_ref):
    @pl.when(pl.program_id(1) == 0)
    def _():
        carry_ref[...] = jnp.zeros_like(carry_ref)

    x = x_ref[0]
    tm = x.shape[0]
    h = _rms(x, g_ref[...])
    row = lax.broadcasted_iota(jnp.int32, h.shape, 0)
    prev = jnp.where(row == 0, carry_ref[0:1, :], pltpu.roll(h, 1, axis=0))
    carry_ref[0:1, :] = h[tm - 1:tm, :]
    dx = prev - h

    def mix(i):
        return h + dx * mu_ref[i:i + 1, :]

    r = _dot(mix(0), wr_ref[...])
    k = _dot(mix(2), wk_ref[...])
    v = _dot(mix(3), wv_ref[...])
    z = w0_ref[...] + _dot(jnp.tanh(_dot(mix(1), w1_ref[...])), w2_ref[...])
    u = -z
    softplus = jnp.maximum(u, 0.0) + jnp.log(1.0 + jnp.exp(-jnp.abs(u)))
    lw_out[0] = -jnp.exp(-softplus - 0.5)
    a = jax.nn.sigmoid(a0_ref[...] + _dot(_dot(mix(4), a1_ref[...]), a2_ref[...]))
    g_out[0] = _dot(jax.nn.sigmoid(_dot(mix(5), g1_ref[...])), g2_ref[...]).astype(g_out.dtype)

    kkr = k * kk_ref[...]
    nrm = jnp.maximum(jnp.sqrt(_dot_exact_rhs(kkr * kkr, seg_ref[...])), 1e-12)
    kk = kkr * _dot_exact_rhs(1.0 / nrm, exp_ref[...])
    r_out[0] = r.astype(r_out.dtype)
    v_out[0] = v.astype(v_out.dtype)
    k_out[0] = (k * (1.0 + (a - 1.0) * ka_ref[...])).astype(k_out.dtype)
    kk_out[0] = kk.astype(kk_out.dtype)
    b_out[0] = (kk * a).astype(b_out.dtype)


def _rwkv_proj(x3d, g, mu, wr, wk, wv, w0, w1, w2, a0, a1, a2, g1, g2, k_k, k_a, seg, exp):
    b, s, c = x3d.shape
    tm = 256
    tok = pl.BlockSpec((1, tm, c), lambda i, j: (i, j, 0))
    consts = [g, mu, wr, wk, wv, w0, w1, w2, a0, a1, a2, g1, g2, k_k, k_a, seg, exp]
    out = [jax.ShapeDtypeStruct((b, s, c), F32 if i == 1 else BF16) for i in range(7)]
    return pl.pallas_call(
        _rwkv_proj_kernel,
        grid=(b, s // tm),
        in_specs=[tok] + [_resident(a.shape) for a in consts],
        out_specs=[tok] * 7,
        out_shape=out,
        scratch_shapes=[pltpu.VMEM((8, c), F32)],
        compiler_params=_params("parallel", "arbitrary"),
        name="rwkv_proj",
    )(x3d, *consts)


def _scan_kernel(r_ref, lw_ref, k_ref, v_ref, kk_ref, b_ref, y_ref, s_ref):
    L = SCAN_CHUNK
    W = SCAN_LANES
    G = W // RWKV_HEAD
    groups = r_ref.shape[2] // W

    @pl.when(pl.program_id(1) == 0)
    def _():
        s_ref[...] = jnp.zeros_like(s_ref)

    rr = lax.broadcasted_iota(jnp.int32, (W, W), 0)
    cc = lax.broadcasted_iota(jnp.int32, (W, W), 1)
    head_bd = rr // RWKV_HEAD == cc // RWKV_HEAD
    bd = head_bd.astype(F32)
    bd16 = head_bd.astype(BF16)
    ar = lax.broadcasted_iota(jnp.int32, (2 * L, 2 * W), 0)
    ac = lax.broadcasted_iota(jnp.int32, (2 * L, 2 * W), 1) % L
    causal = (ac <= jnp.where(ar < L, ar, ar - L - 1)).astype(F32)
    tr = lax.broadcasted_iota(jnp.int32, (L, W), 0)
    tc = lax.broadcasted_iota(jnp.int32, (L, W), 1) % L
    eye = (tc == tr).astype(F32)
    t64r = lax.broadcasted_iota(jnp.int32, (L, L), 0)
    t64c = lax.broadcasted_iota(jnp.int32, (L, L), 1)
    tri = (t64c <= t64r).astype(BF16)

    def stack(x):
        return jnp.concatenate([x.astype(BF16)] * G, axis=0) * bd16

    def rows(*parts):
        return jnp.concatenate(parts, axis=0)

    def cumsum(lw):
        hi = lw.astype(BF16)
        rem = lw - hi.astype(F32)
        mid = rem.astype(BF16)
        lo = (rem - mid.astype(F32)).astype(BF16)
        return (jnp.dot(tri, hi, preferred_element_type=F32)
                + jnp.dot(tri, mid, preferred_element_type=F32)
                + jnp.dot(tri, lo, preferred_element_type=F32))

    def each(fn, *lists):
        return [fn(*args) for args in zip(*lists)]

    def chunk(i, carry):
        sl = pl.ds(pl.multiple_of(i * L, L), L)
        gs = list(range(groups))
        load = lambda ref: [ref[0, sl, g * W:(g + 1) * W] for g in gs]
        f32 = lambda xs_: [t.astype(F32) for t in xs_]
        r, k, v, kk, b = (f32(load(ref)) for ref in (r_ref, k_ref, v_ref, kk_ref, b_ref))
        lw = load(lw_ref)
        state = [s_ref[g] for g in gs]

        cum = each(cumsum, lw)
        x = each(lambda r_, kk_, c_, lw_: rows(r_ * jnp.exp(c_), kk_ * jnp.exp(c_ - lw_)),
                 r, kk, cum, lw)
        inv_gam = each(lambda c_: jnp.exp(-c_), cum)
        ysm = each(lambda b_, k_, ig: rows(stack(b_ * ig), stack(k_ * ig)), b, k, inv_gam)
        vsm = each(stack, v)
        xs = each(_dot_nt, x, state)
        acs = each(lambda x_, y_: _dot_nt(x_, y_) * causal, x, ysm)
        av = each(lambda a, vs: _dot(rows(a[L:, W:], a[:L, W:]), vs), acs, vsm)
        w = each(lambda xs_, av_: xs_[L:] + av_[:L], xs, av)

        m = each(lambda a: -a[L:, :W], acs)
        p = each(lambda m_: eye + m_, m)
        m = each(lambda m_: _dot(m_, stack(m_)), m)
        for _ in range(4):
            mp = each(lambda m_, p_: _dot(rows(m_, p_), stack(m_)), m, p)
            m = each(lambda mp_: mp_[:L], mp)
            p = each(lambda p_, mp_: p_ + mp_[L:], p, mp)
        p = each(lambda m_, p_: p_ + _dot(p_, stack(m_)), m, p)

        sa = each(lambda p_, w_: -_dot(p_, stack(w_)), p, w)
        y = each(lambda xs_, av_, a, sa_: xs_[:L] + av_[L:] + _dot(a[:L, :W], stack(sa_)),
                 xs, av, acs, sa)
        for g in gs:
            y_ref[0, sl, g * W:(g + 1) * W] = y[g]

        def new_state(s_, sa_, v_, b_, k_, c_):
            c_last = c_[L - 1:L, :]
            to_end = jnp.exp(c_last - c_)
            upd = _dot_tn(rows(sa_, v_), rows(b_ * to_end, k_ * to_end))
            return s_ * jnp.exp(c_last) + upd * bd

        for g, s_new in enumerate(each(new_state, state, sa, v, b, k, cum)):
            s_ref[g] = s_new
        return carry

    lax.fori_loop(0, r_ref.shape[1] // L, chunk, 0)


def _rwkv_scan(r, lw, k, v, kk, b):
    bsz, s, c = r.shape
    lt = 256
    tok = pl.BlockSpec((1, lt, c), lambda i, j: (i, j, 0))
    return pl.pallas_call(
        _scan_kernel,
        grid=(bsz, s // lt),
        in_specs=[tok] * 6,
        out_specs=tok,
        out_shape=jax.ShapeDtypeStruct((bsz, s, c), F32),
        scratch_shapes=[pltpu.VMEM((c // SCAN_LANES, SCAN_LANES, SCAN_LANES), F32)],
        compiler_params=_params("parallel", "arbitrary"),
        name="rwkv_scan",
    )(r, lw, k, v, kk, b)


def _rwkv_out_kernel(x_ref, y_ref, r_ref, k_ref, v_ref, g_ref, rk_ref, lnw_ref, lnb_ref,
                     wo_ref, seg_ref, exp_ref, o_ref):
    seg = seg_ref[...]
    exp = exp_ref[...]

    def head_sum(t):
        return _dot_exact_rhs(_dot_exact_rhs(t, seg), exp)

    y = y_ref[...]
    inv_n = 1.0 / RWKV_HEAD
    d = y - head_sum(y) * inv_n
    var = head_sum(d * d) * inv_n
    yn = d * lax.rsqrt(var + GN_EPS) * lnw_ref[...] + lnb_ref[...]
    f32 = lambda ref: ref[...].astype(F32)
    bonus = head_sum(f32(r_ref) * f32(k_ref) * rk_ref[...]) * f32(v_ref)
    o_ref[...] = x_ref[...] + _dot((yn + bonus) * f32(g_ref), wo_ref[...])


def _rwkv_out(x2d, y, r, k, v, g, r_k, ln_w, ln_b, w_o, seg, exp):
    t, c = x2d.shape
    tm = 512
    tok = pl.BlockSpec((tm, c), lambda i: (i, 0))
    consts = [r_k, ln_w, ln_b, w_o, seg, exp]
    return pl.pallas_call(
        _rwkv_out_kernel,
        grid=(t // tm,),
        in_specs=[tok] * 6 + [_resident(a.shape) for a in consts],
        out_specs=tok,
        out_shape=jax.ShapeDtypeStruct((t, c), F32),
        compiler_params=_params("parallel"),
        name="rwkv_out",
    )(x2d, y, r, k, v, g, *consts)


def _gelu_tanh(x):
    return 0.5 * x * (1.0 + jnp.tanh(0.7978845608028654 * (x + 0.044715 * (x * x * x))))


def _gmlp_kernel(x_ref, g_ref, wuv_ref, vn_ref, ws_ref, bs_ref, wo_ref, o_ref, gated_ref):
    x = x_ref[...]
    tm = x.shape[0]
    width = wo_ref.shape[0]
    T = GMLP_CHUNK
    D = GMLP_GROUP_DIM
    h = _rms(x, g_ref[...]).astype(BF16)
    u = _gelu_tanh(jnp.dot(h, wuv_ref[:, :width], preferred_element_type=F32))
    v = _gelu_tanh(jnp.dot(h, wuv_ref[:, width:], preferred_element_type=F32))
    v = _rms(v, vn_ref[...]).astype(BF16)
    row = lax.broadcasted_iota(jnp.int32, (T, T), 0)
    col = lax.broadcasted_iota(jnp.int32, (T, T), 1)
    causal = col <= row
    for grp in range(width // D):
        lanes = slice(grp * D, (grp + 1) * D)
        ws = jnp.where(causal, ws_ref[grp], 0.0).astype(BF16)
        bias = bs_ref[:, grp:grp + 1]
        for ch in range(tm // T):
            rows = slice(ch * T, (ch + 1) * T)
            sv = jnp.dot(ws, v[rows, lanes], preferred_element_type=F32) + bias
            gated_ref[rows, lanes] = (u[rows, lanes] * sv).astype(BF16)
    o_ref[...] = x + jnp.dot(gated_ref[...], wo_ref[...], preferred_element_type=F32)


def _gmlp(x2d, g, w_uv, v_norm, w_s, b_s_t, w_o):
    t, c = x2d.shape
    width = w_o.shape[0]
    tm = 256
    consts = [g, w_uv, v_norm, w_s, b_s_t, w_o]
    return pl.pallas_call(
        _gmlp_kernel,
        grid=(t // tm,),
        in_specs=[pl.BlockSpec((tm, c), lambda i: (i, 0))] + [_resident(a.shape) for a in consts],
        out_specs=pl.BlockSpec((tm, c), lambda i: (i, 0)),
        out_shape=jax.ShapeDtypeStruct((t, c), F32),
        scratch_shapes=[pltpu.VMEM((tm, width), BF16)],
        compiler_params=_params("parallel"),
        name="gmlp",
    )(x2d, *consts)


def _pad_cols(w, n):
    return jnp.pad(w, ((0, 0), (0, n - w.shape[1])))


def _pad_rows(w, n):
    return jnp.pad(w, ((0, n - w.shape[0]), (0, 0)))


def _round_up(n, m):
    return -(-n // m) * m


def _rwkv_layer(x3d, mix_g, mu, w_rkv, w0, w1, w2, a0, a1, a2, g1, g2, k_k, k_a, r_k,
                ln_w, ln_b, w_o):
    b, s, c = x3d.shape
    heads = c // RWKV_HEAD
    head_of = jnp.arange(c) // RWKV_HEAD
    seg = (head_of[:, None] == jnp.arange(LANES)[None, :]).astype(BF16)
    exp = seg.T
    row = lambda p: p.reshape(1, c)
    rd = _round_up(w1.shape[1], LANES)
    ra = _round_up(a1.shape[1], LANES)
    rg = _round_up(g1.shape[1], LANES)
    wb = w_rkv.astype(BF16)
    r, lw, k, v, kk, bb, g = _rwkv_proj(
        x3d, row(mix_g), mu, wb[:, :c], wb[:, c:2 * c], wb[:, 2 * c:], row(w0),
        _pad_cols(w1, rd).astype(BF16), _pad_rows(w2, rd).astype(BF16), row(a0),
        _pad_cols(a1, ra).astype(BF16), _pad_rows(a2, ra).astype(BF16),
        _pad_cols(g1, rg).astype(BF16), _pad_rows(g2, rg).astype(BF16),
        row(k_k), row(k_a), seg, exp)
    y = _rwkv_scan(r, lw, k, v, kk, bb)
    flat = lambda t: t.reshape(b * s, c)
    out = _rwkv_out(flat(x3d), flat(y), flat(r), flat(k), flat(v), flat(g),
                    r_k.reshape(1, heads * RWKV_HEAD), row(ln_w), row(ln_b),
                    w_o.astype(BF16), seg, exp)
    return out.reshape(b, s, c)


def kernel(x, mem, mem_norm, mem_w_kv, ffn1_norm, ffn1_w_in, ffn1_w_out, mix_norm, xattn_norm, xattn_wq, xattn_wo, ffn2_norm, ffn2_w_in, ffn2_w_out, rwkv_mu, rwkv_w_rkv, rwkv_w0, rwkv_w1, rwkv_w2, rwkv_a0, rwkv_a1, rwkv_a2, rwkv_g1, rwkv_g2, rwkv_k_k, rwkv_k_a, rwkv_r_k, rwkv_ln_w, rwkv_ln_b, rwkv_w_o, gmlp_w_uv, gmlp_v_norm, gmlp_w_s, gmlp_b_s, gmlp_w_o, final_norm):
    b, s, c = x.shape
    m = mem.shape[1]
    depth = ffn1_norm.shape[0]
    row = lambda p: p.reshape(1, -1)

    mem_k, mem_v = _mem_kv(mem.reshape(b * m, c), row(mem_norm), mem_w_kv.astype(BF16))
    mem_k = mem_k.reshape(b, m, c)
    mem_v = mem_v.reshape(b, m, c)

    x2d = x.reshape(b * s, c)
    for i in range(depth):
        x2d = _ffn(x2d, row(ffn1_norm[i]), ffn1_w_in[i].astype(BF16), ffn1_w_out[i].astype(BF16))
        j = i // 2
        if i % 2 == 0:
            x3d = _rwkv_layer(
                x2d.reshape(b, s, c), mix_norm[i], rwkv_mu[j], rwkv_w_rkv[j], rwkv_w0[j],
                rwkv_w1[j], rwkv_w2[j], rwkv_a0[j], rwkv_a1[j], rwkv_a2[j], rwkv_g1[j],
                rwkv_g2[j], rwkv_k_k[j], rwkv_k_a[j], rwkv_r_k[j], rwkv_ln_w[j],
                rwkv_ln_b[j], rwkv_w_o[j])
        else:
            x2d = _gmlp(x2d, row(mix_norm[i]), gmlp_w_uv[j].astype(BF16), row(gmlp_v_norm[j]),
                        gmlp_w_s[j], jnp.transpose(gmlp_b_s[j]), gmlp_w_o[j].astype(BF16))
            x3d = x2d.reshape(b, s, c)
        x3d = _xattn(x3d, row(xattn_norm[i]), xattn_wq[i].astype(BF16),
                     xattn_wo[i].astype(BF16), mem_k, mem_v)
        x2d = _ffn(x3d.reshape(b * s, c), row(ffn2_norm[i]), ffn2_w_in[i].astype(BF16),
                   ffn2_w_out[i].astype(BF16),
                   final_g=row(final_norm) if i == depth - 1 else None)
    return x2d.reshape(b, s, c)
```

```python
import functools

import jax
import jax.numpy as jnp
from jax import lax
from jax.experimental import pallas as pl
from jax.experimental.pallas import tpu as pltpu

F32 = jnp.float32
BF16 = jnp.bfloat16

RMS_EPS = 1e-6
GN_EPS = 64e-5
RWKV_HEAD = 64
GMLP_CHUNK = 128
GMLP_GROUP_DIM = 128
XATTN_HEADS = 4

LANES = 128
SCAN_CHUNK = 64
SCAN_LANES = 256
VMEM_LIMIT = 52 * 1024 * 1024


def _resident(shape):
    nd = len(shape)
    return pl.BlockSpec(shape, lambda *_: (0,) * nd, pipeline_mode=pl.Buffered(1))


def _params(*sem):
    return pltpu.CompilerParams(dimension_semantics=sem, vmem_limit_bytes=VMEM_LIMIT)


def _rms(x, g):
    return x * lax.rsqrt(jnp.mean(x * x, axis=-1, keepdims=True) + RMS_EPS) * g


def _dot(a, b):
    return jnp.dot(a.astype(BF16), b.astype(BF16), preferred_element_type=F32)


def _dot_nt(a, b):
    return lax.dot_general(a.astype(BF16), b.astype(BF16), (((1,), (1,)), ((), ())),
                           preferred_element_type=F32)


def _dot_tn(a, b):
    return lax.dot_general(a.astype(BF16), b.astype(BF16), (((0,), (0,)), ((), ())),
                           preferred_element_type=F32)


def _split2(x):
    hi = x.astype(BF16)
    lo = (x - hi.astype(F32)).astype(BF16)
    return hi, lo


def _head_ones(width):
    r = lax.broadcasted_iota(jnp.int32, (width, width), 0) // RWKV_HEAD
    c = lax.broadcasted_iota(jnp.int32, (width, width), 1) // RWKV_HEAD
    return (r == c).astype(BF16)


def _head_sums(x, ones):
    w = ones.shape[0]
    hi, lo = _split2(x)
    parts = [jnp.dot(hi[:, i:i + w], ones, preferred_element_type=F32)
             + jnp.dot(lo[:, i:i + w], ones, preferred_element_type=F32)
             for i in range(0, x.shape[1], w)]
    return jnp.concatenate(parts, axis=1)


def _mem_kv_kernel(m_ref, g_ref, w_ref, k_ref, v_ref):
    c = m_ref.shape[-1]
    h = _rms(m_ref[...], g_ref[...])
    kv = _dot(h, w_ref[...])
    k_ref[...] = kv[:, :c].astype(BF16)
    v_ref[...] = kv[:, c:].astype(BF16)


def _mem_kv(mem2d, g, w_kv):
    t, c = mem2d.shape
    tm = 512
    out = jax.ShapeDtypeStruct((t, c), BF16)
    return pl.pallas_call(
        _mem_kv_kernel,
        grid=(t // tm,),
        in_specs=[pl.BlockSpec((tm, c), lambda i: (i, 0)),
                  _resident((1, c)), _resident((c, 2 * c))],
        out_specs=[pl.BlockSpec((tm, c), lambda i: (i, 0))] * 2,
        out_shape=[out, out],
        compiler_params=_params("parallel"),
        name="mem_kv",
    )(mem2d, g, w_kv)


def _ffn_kernel(x_ref, g_ref, win_ref, wout_ref, *rest, f_chunk, final):
    o_ref = rest[-1]
    d_ff = wout_ref.shape[0]
    x = x_ref[...]
    h = _rms(x, g_ref[...]).astype(BF16)
    acc = jnp.zeros_like(x)
    for lo in range(0, d_ff, f_chunk):
        gate = jnp.dot(h, win_ref[:, lo:lo + f_chunk], preferred_element_type=F32)
        up = jnp.dot(h, win_ref[:, d_ff + lo:d_ff + lo + f_chunk], preferred_element_type=F32)
        act = (gate * jax.nn.sigmoid(gate) * up).astype(BF16)
        acc = acc + jnp.dot(act, wout_ref[lo:lo + f_chunk, :], preferred_element_type=F32)
    y = x + 0.5 * acc
    if final:
        y = _rms(y, rest[0][...])
    o_ref[...] = y


def _ffn(x2d, g, w_in, w_out, final_g=None):
    t, c = x2d.shape
    d_ff = w_out.shape[0]
    tm = 512
    f_chunk = d_ff // 2
    final = final_g is not None
    in_specs = [pl.BlockSpec((tm, c), lambda i: (i, 0)),
                _resident((1, c)), _resident((c, 2 * d_ff)), _resident((d_ff, c))]
    args = [x2d, g, w_in, w_out]
    if final:
        in_specs.append(_resident((1, c)))
        args.append(final_g)
    return pl.pallas_call(
        functools.partial(_ffn_kernel, f_chunk=f_chunk, final=final),
        grid=(t // tm,),
        in_specs=in_specs,
        out_specs=pl.BlockSpec((tm, c), lambda i: (i, 0)),
        out_shape=jax.ShapeDtypeStruct((t, c), F32),
        compiler_params=_params("parallel"),
        name="ffn_final" if final else "ffn",
    )(*args)


def _xattn_kernel(x_ref, g_ref, wq_ref, wo_ref, k_ref, v_ref, o_ref, att_ref):
    x = x_ref[0]
    c = x.shape[-1]
    hd = c // XATTN_HEADS
    h = _rms(x, g_ref[...])
    q = (_dot(h, wq_ref[...]) * (hd ** -0.5)).astype(BF16)
    for i in range(XATTN_HEADS):
        sl = slice(i * hd, (i + 1) * hd)
        s = _dot_nt(q[:, sl], k_ref[0, :, sl])
        p = jnp.exp(s - jnp.max(s, axis=-1, keepdims=True))
        l = jnp.sum(p, axis=-1, keepdims=True)
        att_ref[:, sl] = (_dot(p, v_ref[0, :, sl]) / l).astype(BF16)
    o_ref[0] = x + jnp.dot(att_ref[...], wo_ref[...], preferred_element_type=F32)


def _xattn(x3d, g, wq, wo, mem_k, mem_v):
    b, s, c = x3d.shape
    m = mem_k.shape[1]
    tm = 1024
    return pl.pallas_call(
        _xattn_kernel,
        grid=(b, s // tm),
        in_specs=[pl.BlockSpec((1, tm, c), lambda i, j: (i, j, 0)),
                  _resident((1, c)), _resident((c, c)), _resident((c, c)),
                  pl.BlockSpec((1, m, c), lambda i, j: (i, 0, 0)),
                  pl.BlockSpec((1, m, c), lambda i, j: (i, 0, 0))],
        out_specs=pl.BlockSpec((1, tm, c), lambda i, j: (i, j, 0)),
        out_shape=jax.ShapeDtypeStruct((b, s, c), F32),
        scratch_shapes=[pltpu.VMEM((tm, c), BF16)],
        compiler_params=_params("parallel", "parallel"),
        name="xattn",
    )(x3d, g, wq, wo, mem_k, mem_v)


def _rwkv_proj_kernel(x_ref, g_ref, mu_ref, wr_ref, wk_ref, wv_ref, w0_ref, w1_ref, w2_ref,
                      a0_ref, a1_ref, a2_ref, g1_ref, g2_ref, kk_ref, ka_ref,
                      r_out, lw_out, k_out, v_out, kk_out, b_out, g_out, carry_ref):
    @pl.when(pl.program_id(1) == 0)
    def _():
        carry_ref[...] = jnp.zeros_like(carry_ref)

    x = x_ref[0]
    tm = x.shape[0]
    h = _rms(x, g_ref[...])
    row = lax.broadcasted_iota(jnp.int32, h.shape, 0)
    prev = jnp.where(row == 0, carry_ref[0:1, :], pltpu.roll(h, 1, axis=0))
    carry_ref[0:1, :] = h[tm - 1:tm, :]
    dx = prev - h

    def mix(i):
        return h + dx * mu_ref[i:i + 1, :]

    r = _dot(mix(0), wr_ref[...])
    k = _dot(mix(2), wk_ref[...])
    v = _dot(mix(3), wv_ref[...])
    z = w0_ref[...] + _dot(jnp.tanh(_dot(mix(1), w1_ref[...])), w2_ref[...])
    u = -z
    softplus = jnp.maximum(u, 0.0) + jnp.log(1.0 + jnp.exp(-jnp.abs(u)))
    lw_out[0] = -jnp.exp(-softplus - 0.5)
    a = jax.nn.sigmoid(a0_ref[...] + _dot(_dot(mix(4), a1_ref[...]), a2_ref[...]))
    g_out[0] = _dot(jax.nn.sigmoid(_dot(mix(5), g1_ref[...])), g2_ref[...]).astype(g_out.dtype)

    kkr = k * kk_ref[...]
    kk = kkr * lax.rsqrt(jnp.maximum(_head_sums(kkr * kkr, _head_ones(SCAN_LANES)), 1e-24))
    r_out[0] = r.astype(r_out.dtype)
    v_out[0] = v.astype(v_out.dtype)
    k_out[0] = (k * (1.0 + (a - 1.0) * ka_ref[...])).astype(k_out.dtype)
    kk_out[0] = kk.astype(kk_out.dtype)
    b_out[0] = (kk * a).astype(b_out.dtype)


def _rwkv_proj(x3d, g, mu, w_rkv, w0, w1, w2, a0, a1, a2, g1, g2, k_k, k_a):
    b, s, c = x3d.shape
    tm = 512
    tok = pl.BlockSpec((1, tm, c), lambda i, j: (i, j, 0))
    rkv = [pl.BlockSpec((c, c), lambda i, j, n=n: (0, n), pipeline_mode=pl.Buffered(1))
           for n in range(3)]
    consts = [w0, w1, w2, a0, a1, a2, g1, g2, k_k, k_a]
    out = [jax.ShapeDtypeStruct((b, s, c), F32 if i == 1 else BF16) for i in range(7)]
    return pl.pallas_call(
        _rwkv_proj_kernel,
        grid=(b, s // tm),
        in_specs=([tok, _resident(g.shape), _resident(mu.shape)] + rkv
                  + [_resident(a.shape) for a in consts]),
        out_specs=[tok] * 7,
        out_shape=out,
        scratch_shapes=[pltpu.VMEM((8, c), F32)],
        compiler_params=_params("parallel", "arbitrary"),
        name="rwkv_proj",
    )(x3d, g, mu, w_rkv, w_rkv, w_rkv, *consts)


def _scan_kernel(r_ref, lw_ref, k_ref, v_ref, kk_ref, b_ref, y_ref, s_ref):
    L = SCAN_CHUNK
    W = SCAN_LANES
    G = W // RWKV_HEAD
    groups = r_ref.shape[2] // W

    @pl.when(pl.program_id(1) == 0)
    def _():
        s_ref[...] = jnp.zeros_like(s_ref)

    rr = lax.broadcasted_iota(jnp.int32, (W, W), 0)
    cc = lax.broadcasted_iota(jnp.int32, (W, W), 1)
    head_bd = rr // RWKV_HEAD == cc // RWKV_HEAD
    bd = head_bd.astype(F32)
    bd16 = head_bd.astype(BF16)
    ar = lax.broadcasted_iota(jnp.int32, (2 * L, 2 * W), 0)
    ac = lax.broadcasted_iota(jnp.int32, (2 * L, 2 * W), 1) % L
    causal = (ac <= jnp.where(ar < L, ar, ar - L - 1)).astype(F32)
    tr = lax.broadcasted_iota(jnp.int32, (L, W), 0)
    tc = lax.broadcasted_iota(jnp.int32, (L, W), 1) % L
    eye = (tc == tr).astype(F32)
    t64r = lax.broadcasted_iota(jnp.int32, (L, L), 0)
    t64c = lax.broadcasted_iota(jnp.int32, (L, L), 1)
    tri = (t64c <= t64r).astype(BF16)

    def stack(x):
        return jnp.concatenate([x.astype(BF16)] * G, axis=0) * bd16

    def rows(*parts):
        return jnp.concatenate(parts, axis=0)

    def cumsum(lw):
        hi, lo = _split2(lw)
        return (jnp.dot(tri, hi, preferred_element_type=F32)
                + jnp.dot(tri, lo, preferred_element_type=F32))

    def each(fn, *lists):
        return [fn(*args) for args in zip(*lists)]

    def chunk(i, carry):
        sl = pl.ds(pl.multiple_of(i * L, L), L)
        streams = [(bi, g) for bi in range(r_ref.shape[0]) for g in range(groups)]
        load = lambda ref: [ref[bi, sl, g * W:(g + 1) * W] for bi, g in streams]
        f32 = lambda xs_: [t.astype(F32) for t in xs_]
        r, k, v, kk, b = (f32(load(ref)) for ref in (r_ref, k_ref, v_ref, kk_ref, b_ref))
        lw = load(lw_ref)
        state = [s_ref[n] for n in range(len(streams))]

        cum = each(cumsum, lw)
        x = each(lambda r_, kk_, c_, lw_: rows(r_ * jnp.exp(c_), kk_ * jnp.exp(c_ - lw_)),
                 r, kk, cum, lw)
        inv_gam = each(lambda c_: jnp.exp(-c_), cum)
        ysm = each(lambda b_, k_, ig: rows(stack(b_ * ig), stack(k_ * ig)), b, k, inv_gam)
        vsm = each(stack, v)
        xs = each(_dot_nt, x, state)
        acs = each(lambda x_, y_: _dot_nt(x_, y_) * causal, x, ysm)
        av = each(lambda a, vs: _dot(rows(a[L:, W:], a[:L, W:]), vs), acs, vsm)
        w = each(lambda xs_, av_: xs_[L:] + av_[:L], xs, av)

        m = each(lambda a: -a[L:, :W], acs)
        p = each(lambda m_: eye + m_, m)
        m = each(lambda m_: _dot(m_, stack(m_)), m)
        for _ in range(4):
            mp = each(lambda m_, p_: _dot(rows(m_, p_), stack(m_)), m, p)
            m = each(lambda mp_: mp_[:L], mp)
            p = each(lambda p_, mp_: p_ + mp_[L:], p, mp)
        p = each(lambda m_, p_: p_ + _dot(p_, stack(m_)), m, p)

        sa = each(lambda p_, w_: -_dot(p_, stack(w_)), p, w)
        y = each(lambda xs_, av_, a, sa_: xs_[:L] + av_[L:] + _dot(a[:L, :W], stack(sa_)),
                 xs, av, acs, sa)
        for (bi, g), y_ in zip(streams, y):
            y_ref[bi, sl, g * W:(g + 1) * W] = y_

        def new_state(s_, sa_, v_, b_, k_, c_):
            c_last = c_[L - 1:L, :]
            to_end = jnp.exp(c_last - c_)
            upd = _dot_tn(rows(sa_, v_), rows(b_ * to_end, k_ * to_end))
            return s_ * jnp.exp(c_last) + upd * bd

        for n, s_new in enumerate(each(new_state, state, sa, v, b, k, cum)):
            s_ref[n] = s_new
        return carry

    lax.fori_loop(0, r_ref.shape[1] // L, chunk, 0)


def _rwkv_scan(r, lw, k, v, kk, b):
    bsz, s, c = r.shape
    lt = 256
    nb = 2
    tok = pl.BlockSpec((nb, lt, c), lambda i, j: (i, j, 0))
    return pl.pallas_call(
        _scan_kernel,
        grid=(bsz // nb, s // lt),
        in_specs=[tok] * 6,
        out_specs=tok,
        out_shape=jax.ShapeDtypeStruct((bsz, s, c), F32),
        scratch_shapes=[pltpu.VMEM((nb * c // SCAN_LANES, SCAN_LANES, SCAN_LANES), F32)],
        compiler_params=_params("parallel", "arbitrary"),
        name="rwkv_scan",
    )(r, lw, k, v, kk, b)


def _rwkv_out_kernel(x_ref, y_ref, r_ref, k_ref, v_ref, g_ref, rk_ref, lnw_ref, lnb_ref,
                     wo_ref, o_ref):
    ones = _head_ones(SCAN_LANES)
    head_sum = lambda t: _head_sums(t, ones)
    y = y_ref[...]
    inv_n = 1.0 / RWKV_HEAD
    d = y - head_sum(y) * inv_n
    var = head_sum(d * d) * inv_n
    yn = d * lax.rsqrt(var + GN_EPS) * lnw_ref[...] + lnb_ref[...]
    f32 = lambda ref: ref[...].astype(F32)
    bonus = head_sum(f32(r_ref) * f32(k_ref) * rk_ref[...]) * f32(v_ref)
    o_ref[...] = x_ref[...] + _dot((yn + bonus) * f32(g_ref), wo_ref[...])


def _rwkv_out(x2d, y, r, k, v, g, r_k, ln_w, ln_b, w_o):
    t, c = x2d.shape
    tm = 1024
    tok = pl.BlockSpec((tm, c), lambda i: (i, 0))
    consts = [r_k, ln_w, ln_b, w_o]
    return pl.pallas_call(
        _rwkv_out_kernel,
        grid=(t // tm,),
        in_specs=[tok] * 6 + [_resident(a.shape) for a in consts],
        out_specs=tok,
        out_shape=jax.ShapeDtypeStruct((t, c), F32),
        compiler_params=_params("parallel"),
        name="rwkv_out",
    )(x2d, y, r, k, v, g, *consts)


def _gelu_tanh(x):
    return 0.5 * x * (1.0 + jnp.tanh(0.7978845608028654 * (x + 0.044715 * (x * x * x))))


def _gmlp_kernel(x_ref, g_ref, wuv_ref, vn_ref, ws_ref, bs_ref, wo_ref, o_ref, gated_ref):
    x = x_ref[...]
    tm = x.shape[0]
    width = wo_ref.shape[0]
    T = GMLP_CHUNK
    D = GMLP_GROUP_DIM
    h = _rms(x, g_ref[...]).astype(BF16)
    u = _gelu_tanh(jnp.dot(h, wuv_ref[:, :width], preferred_element_type=F32))
    v = _gelu_tanh(jnp.dot(h, wuv_ref[:, width:], preferred_element_type=F32))
    v = _rms(v, vn_ref[...]).astype(BF16)
    row = lax.broadcasted_iota(jnp.int32, (T, T), 0)
    col = lax.broadcasted_iota(jnp.int32, (T, T), 1)
    causal = col <= row
    for grp in range(width // D):
        lanes = slice(grp * D, (grp + 1) * D)
        ws = jnp.where(causal, ws_ref[grp], 0.0).astype(BF16)
        bias = bs_ref[:, grp:grp + 1]
        for ch in range(tm // T):
            rows = slice(ch * T, (ch + 1) * T)
            sv = jnp.dot(ws, v[rows, lanes], preferred_element_type=F32) + bias
            gated_ref[rows, lanes] = (u[rows, lanes] * sv).astype(BF16)
    o_ref[...] = x + jnp.dot(gated_ref[...], wo_ref[...], preferred_element_type=F32)


def _gmlp(x2d, g, w_uv, v_norm, w_s, b_s_t, w_o):
    t, c = x2d.shape
    width = w_o.shape[0]
    tm = 512
    consts = [g, w_uv, v_norm, w_s, b_s_t, w_o]
    return pl.pallas_call(
        _gmlp_kernel,
        grid=(t // tm,),
        in_specs=[pl.BlockSpec((tm, c), lambda i: (i, 0))] + [_resident(a.shape) for a in consts],
        out_specs=pl.BlockSpec((tm, c), lambda i: (i, 0)),
        out_shape=jax.ShapeDtypeStruct((t, c), F32),
        scratch_shapes=[pltpu.VMEM((tm, width), BF16)],
        compiler_params=_params("parallel"),
        name="gmlp",
    )(x2d, *consts)


def _pad_cols(w, n):
    return jnp.pad(w, ((0, 0), (0, n - w.shape[1])))


def _pad_rows(w, n):
    return jnp.pad(w, ((0, n - w.shape[0]), (0, 0)))


def _round_up(n, m):
    return -(-n // m) * m


def _rwkv_layer(x3d, mix_g, mu, w_rkv, w0, w1, w2, a0, a1, a2, g1, g2, k_k, k_a, r_k,
                ln_w, ln_b, w_o):
    b, s, c = x3d.shape
    row = lambda p: p.reshape(1, c)
    rd = _round_up(w1.shape[1], LANES)
    ra = _round_up(a1.shape[1], LANES)
    rg = _round_up(g1.shape[1], LANES)
    r, lw, k, v, kk, bb, g = _rwkv_proj(
        x3d, row(mix_g), mu, w_rkv.astype(BF16), row(w0),
        _pad_cols(w1, rd).astype(BF16), _pad_rows(w2, rd).astype(BF16), row(a0),
        _pad_cols(a1, ra).astype(BF16), _pad_rows(a2, ra).astype(BF16),
        _pad_cols(g1, rg).astype(BF16), _pad_rows(g2, rg).astype(BF16),
        row(k_k), row(k_a))
    y = _rwkv_scan(r, lw, k, v, kk, bb)
    flat = lambda t: t.reshape(b * s, c)
    out = _rwkv_out(flat(x3d), flat(y), flat(r), flat(k), flat(v), flat(g),
                    row(r_k), row(ln_w), row(ln_b), w_o.astype(BF16))
    return out.reshape(b, s, c)


def kernel(x, mem, mem_norm, mem_w_kv, ffn1_norm, ffn1_w_in, ffn1_w_out, mix_norm, xattn_norm, xattn_wq, xattn_wo, ffn2_norm, ffn2_w_in, ffn2_w_out, rwkv_mu, rwkv_w_rkv, rwkv_w0, rwkv_w1, rwkv_w2, rwkv_a0, rwkv_a1, rwkv_a2, rwkv_g1, rwkv_g2, rwkv_k_k, rwkv_k_a, rwkv_r_k, rwkv_ln_w, rwkv_ln_b, rwkv_w_o, gmlp_w_uv, gmlp_v_norm, gmlp_w_s, gmlp_b_s, gmlp_w_o, final_norm):
    b, s, c = x.shape
    m = mem.shape[1]
    depth = ffn1_norm.shape[0]
    row = lambda p: p.reshape(1, -1)

    mem_k, mem_v = _mem_kv(mem.reshape(b * m, c), row(mem_norm), mem_w_kv.astype(BF16))
    mem_k = mem_k.reshape(b, m, c)
    mem_v = mem_v.reshape(b, m, c)

    x2d = x.reshape(b * s, c)
    for i in range(depth):
        x2d = _ffn(x2d, row(ffn1_norm[i]), ffn1_w_in[i].astype(BF16), ffn1_w_out[i].astype(BF16))
        j = i // 2
        if i % 2 == 0:
            x3d = _rwkv_layer(
                x2d.reshape(b, s, c), mix_norm[i], rwkv_mu[j], rwkv_w_rkv[j], rwkv_w0[j],
                rwkv_w1[j], rwkv_w2[j], rwkv_a0[j], rwkv_a1[j], rwkv_a2[j], rwkv_g1[j],
                rwkv_g2[j], rwkv_k_k[j], rwkv_k_a[j], rwkv_r_k[j], rwkv_ln_w[j],
                rwkv_ln_b[j], rwkv_w_o[j])
        else:
            x2d = _gmlp(x2d, row(mix_norm[i]), gmlp_w_uv[j].astype(BF16), row(gmlp_v_norm[j]),
                        gmlp_w_s[j], jnp.transpose(gmlp_b_s[j]), gmlp_w_o[j].astype(BF16))
            x3d = x2d.reshape(b, s, c)
        x3d = _xattn(x3d, row(xattn_norm[i]), xattn_wq[i].astype(BF16),
                     xattn_wo[i].astype(BF16), mem_k, mem_v)
        x2d = _ffn(x3d.reshape(b * s, c), row(ffn2_norm[i]), ffn2_w_in[i].astype(BF16),
                   ffn2_w_out[i].astype(BF16),
                   final_g=row(final_norm) if i == depth - 1 else None)
    return x2d.reshape(b, s, c)
```

```python
import functools

import jax
import jax.numpy as jnp
from jax import lax
from jax.experimental import pallas as pl
from jax.experimental.pallas import tpu as pltpu

F32 = jnp.float32
BF16 = jnp.bfloat16

RMS_EPS = 1e-6
GN_EPS = 64e-5
RWKV_HEAD = 64
GMLP_CHUNK = 128
GMLP_GROUP_DIM = 128
XATTN_HEADS = 4

LANES = 128
SCAN_CHUNK = 64
MXU_WIDTH = 256
SCAN_LANES = MXU_WIDTH
VMEM_LIMIT = 52 * 1024 * 1024


def _resident(shape):
    nd = len(shape)
    return pl.BlockSpec(shape, lambda *_: (0,) * nd, pipeline_mode=pl.Buffered(1))


def _params(*sem):
    return pltpu.CompilerParams(dimension_semantics=sem, vmem_limit_bytes=VMEM_LIMIT)


def _rms(x, g):
    return x * lax.rsqrt(jnp.mean(x * x, axis=-1, keepdims=True) + RMS_EPS) * g


def _dot(a, b):
    return jnp.dot(a.astype(BF16), b.astype(BF16), preferred_element_type=F32)


def _dot_nt(a, b):
    return lax.dot_general(a.astype(BF16), b.astype(BF16), (((1,), (1,)), ((), ())),
                           preferred_element_type=F32)


def _dot_tn(a, b):
    return lax.dot_general(a.astype(BF16), b.astype(BF16), (((0,), (0,)), ((), ())),
                           preferred_element_type=F32)


def _split2(x):
    hi = x.astype(BF16)
    lo = (x - hi.astype(F32)).astype(BF16)
    return hi, lo


def _head_ones(width):
    r = lax.broadcasted_iota(jnp.int32, (width, width), 0) // RWKV_HEAD
    c = lax.broadcasted_iota(jnp.int32, (width, width), 1) // RWKV_HEAD
    return (r == c).astype(BF16)


def _head_sums(x, ones):
    w = ones.shape[0]
    hi, lo = _split2(x)
    parts = [jnp.dot(hi[:, i:i + w], ones, preferred_element_type=F32)
             + jnp.dot(lo[:, i:i + w], ones, preferred_element_type=F32)
             for i in range(0, x.shape[1], w)]
    return jnp.concatenate(parts, axis=1)


def _mem_fold_kernel(m_ref, g_ref, wkv_ref, wq_ref, wo_ref, *outs):
    c = m_ref.shape[-1]
    hd = c // XATTN_HEADS
    n_mem = m_ref.shape[1]
    kv = _dot(_rms(m_ref[0], g_ref[...]), wkv_ref[...])
    k = kv[:, :c].astype(BF16)
    v = kv[:, c:].astype(BF16)
    for layer in range(wq_ref.shape[0]):
        qk_ref, vo_ref = outs[2 * layer], outs[2 * layer + 1]
        for i in range(XATTN_HEADS):
            sl = slice(i * hd, (i + 1) * hd)
            qk_ref[0, :, i * n_mem:(i + 1) * n_mem] = _dot_nt(wq_ref[layer, :, sl], k[:, sl]).astype(BF16)
            vo_ref[0, i * n_mem:(i + 1) * n_mem, :] = _dot(v[:, sl], wo_ref[layer, sl, :]).astype(BF16)


def _mem_fold(mem, g, w_kv, wq, wo):
    b, m, c = mem.shape
    depth = wq.shape[0]
    hm = XATTN_HEADS * m
    shapes = [jax.ShapeDtypeStruct((b, c, hm), BF16), jax.ShapeDtypeStruct((b, hm, c), BF16)] * depth
    specs = [pl.BlockSpec((1, c, hm), lambda i: (i, 0, 0)), pl.BlockSpec((1, hm, c), lambda i: (i, 0, 0))] * depth
    outs = pl.pallas_call(
        _mem_fold_kernel,
        grid=(b,),
        in_specs=[pl.BlockSpec((1, m, c), lambda i: (i, 0, 0)), _resident(g.shape),
                  _resident(w_kv.shape), _resident(wq.shape), _resident(wo.shape)],
        out_specs=specs,
        out_shape=shapes,
        compiler_params=_params("parallel"),
        name="mem_fold",
    )(mem, g, w_kv, wq, wo)
    return [(outs[2 * i], outs[2 * i + 1]) for i in range(depth)]


def _ffn_kernel(x_ref, g_ref, win_ref, wout_ref, *rest, f_chunk, final):
    o_ref = rest[-1]
    d_ff = wout_ref.shape[0]
    x = x_ref[...]
    h = _rms(x, g_ref[...]).astype(BF16)
    acc = jnp.zeros_like(x)
    for lo in range(0, d_ff, f_chunk):
        gate = jnp.dot(h, win_ref[:, lo:lo + f_chunk], preferred_element_type=F32)
        up = jnp.dot(h, win_ref[:, d_ff + lo:d_ff + lo + f_chunk], preferred_element_type=F32)
        act = (gate * jax.nn.sigmoid(gate) * up).astype(BF16)
        acc = acc + jnp.dot(act, wout_ref[lo:lo + f_chunk, :], preferred_element_type=F32)
    y = x + 0.5 * acc
    if final:
        y = _rms(y, rest[0][...])
    o_ref[...] = y


def _ffn(x2d, g, w_in, w_out, final_g=None):
    t, c = x2d.shape
    d_ff = w_out.shape[0]
    tm = 1024
    f_chunk = MXU_WIDTH
    assert d_ff % f_chunk == 0
    final = final_g is not None
    in_specs = [pl.BlockSpec((tm, c), lambda i: (i, 0)),
                _resident((1, c)), _resident((c, 2 * d_ff)), _resident((d_ff, c))]
    args = [x2d, g, w_in, w_out]
    if final:
        in_specs.append(_resident((1, c)))
        args.append(final_g)
    return pl.pallas_call(
        functools.partial(_ffn_kernel, f_chunk=f_chunk, final=final),
        grid=(t // tm,),
        in_specs=in_specs,
        out_specs=pl.BlockSpec((tm, c), lambda i: (i, 0)),
        out_shape=jax.ShapeDtypeStruct((t, c), F32),
        compiler_params=_params("parallel"),
        name="ffn_final" if final else "ffn",
    )(*args)


def _xattn_kernel(x_ref, g_ref, qk_ref, vo_ref, o_ref):
    x = x_ref[0]
    hm = qk_ref.shape[2]
    n_mem = hm // XATTN_HEADS
    scale = (x.shape[-1] // XATTN_HEADS) ** -0.5
    s = _dot(_rms(x, g_ref[...]), qk_ref[0]) * scale
    probs = []
    for i in range(XATTN_HEADS):
        si = s[:, i * n_mem:(i + 1) * n_mem]
        p = jnp.exp(si - jnp.max(si, axis=-1, keepdims=True))
        probs.append((p / jnp.sum(p, axis=-1, keepdims=True)).astype(BF16))
    o_ref[0] = x + jnp.dot(jnp.concatenate(probs, axis=1), vo_ref[0], preferred_element_type=F32)


def _xattn(x3d, g, qk, vo):
    b, s, c = x3d.shape
    hm = qk.shape[2]
    tm = 1024
    return pl.pallas_call(
        _xattn_kernel,
        grid=(b, s // tm),
        in_specs=[pl.BlockSpec((1, tm, c), lambda i, j: (i, j, 0)), _resident((1, c)),
                  pl.BlockSpec((1, c, hm), lambda i, j: (i, 0, 0)),
                  pl.BlockSpec((1, hm, c), lambda i, j: (i, 0, 0))],
        out_specs=pl.BlockSpec((1, tm, c), lambda i, j: (i, j, 0)),
        out_shape=jax.ShapeDtypeStruct((b, s, c), F32),
        compiler_params=_params("parallel", "parallel"),
        name="xattn",
    )(x3d, g, qk, vo)


def _rwkv_proj_kernel(x_ref, g_ref, mu_ref, wr_ref, wk_ref, wv_ref, w0_ref, w1_ref, w2_ref,
                      a0_ref, a1_ref, a2_ref, g1_ref, g2_ref, kk_ref, ka_ref,
                      r_out, lw_out, k_out, v_out, kk_out, b_out, g_out, carry_ref):
    @pl.when(pl.program_id(1) == 0)
    def _():
        carry_ref[...] = jnp.zeros_like(carry_ref)

    x = x_ref[0]
    tm = x.shape[0]
    h = _rms(x, g_ref[...])
    row = lax.broadcasted_iota(jnp.int32, h.shape, 0)
    prev = jnp.where(row == 0, carry_ref[0:1, :], pltpu.roll(h, 1, axis=0))
    carry_ref[0:1, :] = h[tm - 1:tm, :]
    dx = prev - h

    def mix(i):
        return h + dx * mu_ref[i:i + 1, :]

    dw = jnp.tanh(_dot(mix(1), w1_ref[...]))
    da = _dot(mix(4), a1_ref[...])
    dg = jax.nn.sigmoid(_dot(mix(5), g1_ref[...]))
    k = _dot(mix(2), wk_ref[...])
    z = w0_ref[...] + _dot(dw, w2_ref[...])
    a = jax.nn.sigmoid(a0_ref[...] + _dot(da, a2_ref[...]))
    g_out[0] = _dot(dg, g2_ref[...]).astype(g_out.dtype)
    r = _dot(mix(0), wr_ref[...])
    kkr = k * kk_ref[...]
    kk = kkr * lax.rsqrt(jnp.maximum(_head_sums(kkr * kkr, _head_ones(SCAN_LANES)), 1e-24))
    v = _dot(mix(3), wv_ref[...])
    u = -z
    softplus = jnp.maximum(u, 0.0) + jnp.log(1.0 + jnp.exp(-jnp.abs(u)))
    lw_out[0] = -jnp.exp(-softplus - 0.5)
    r_out[0] = r.astype(r_out.dtype)
    v_out[0] = v.astype(v_out.dtype)
    k_out[0] = (k * (1.0 + (a - 1.0) * ka_ref[...])).astype(k_out.dtype)
    kk_out[0] = kk.astype(kk_out.dtype)
    b_out[0] = (kk * a).astype(b_out.dtype)


def _rwkv_proj(x3d, g, mu, w_rkv, w0, w1, w2, a0, a1, a2, g1, g2, k_k, k_a):
    b, s, c = x3d.shape
    tm = 512
    tok = pl.BlockSpec((1, tm, c), lambda i, j: (i, j, 0))
    rkv = [pl.BlockSpec((c, c), lambda i, j, n=n: (0, n), pipeline_mode=pl.Buffered(1))
           for n in range(3)]
    consts = [w0, w1, w2, a0, a1, a2, g1, g2, k_k, k_a]
    out = [jax.ShapeDtypeStruct((b, s, c), F32 if i == 1 else BF16) for i in range(7)]
    return pl.pallas_call(
        _rwkv_proj_kernel,
        grid=(b, s // tm),
        in_specs=([tok, _resident(g.shape), _resident(mu.shape)] + rkv
                  + [_resident(a.shape) for a in consts]),
        out_specs=[tok] * 7,
        out_shape=out,
        scratch_shapes=[pltpu.VMEM((8, c), F32)],
        compiler_params=_params("parallel", "arbitrary"),
        name="rwkv_proj",
    )(x3d, g, mu, w_rkv, w_rkv, w_rkv, *consts)


def _scan_kernel(r_ref, lw_ref, k_ref, v_ref, kk_ref, b_ref, y_ref, s_ref):
    L = SCAN_CHUNK
    W = SCAN_LANES
    G = W // RWKV_HEAD
    groups = r_ref.shape[2] // W

    @pl.when(pl.program_id(1) == 0)
    def _():
        s_ref[...] = jnp.zeros_like(s_ref)

    rr = lax.broadcasted_iota(jnp.int32, (W, W), 0)
    cc = lax.broadcasted_iota(jnp.int32, (W, W), 1)
    head_bd = rr // RWKV_HEAD == cc // RWKV_HEAD
    bd = head_bd.astype(F32)
    bd16 = head_bd.astype(BF16)
    ar = lax.broadcasted_iota(jnp.int32, (2 * L, 2 * W), 0)
    ac = lax.broadcasted_iota(jnp.int32, (2 * L, 2 * W), 1) % L
    causal = (ac <= jnp.where(ar < L, ar, ar - L - 1)).astype(F32)
    tr = lax.broadcasted_iota(jnp.int32, (L, W), 0)
    tc = lax.broadcasted_iota(jnp.int32, (L, W), 1) % L
    eye = (tc == tr).astype(F32)
    t64r = lax.broadcasted_iota(jnp.int32, (L, L), 0)
    t64c = lax.broadcasted_iota(jnp.int32, (L, L), 1)
    tri = (t64c <= t64r).astype(BF16)

    def stack(x):
        return jnp.concatenate([x.astype(BF16)] * G, axis=0) * bd16

    def rows(*parts):
        return jnp.concatenate(parts, axis=0)

    def cumsum(lw):
        hi, lo = _split2(lw)
        return (jnp.dot(tri, hi, preferred_element_type=F32)
                + jnp.dot(tri, lo, preferred_element_type=F32))

    def each(fn, *lists):
        return [fn(*args) for args in zip(*lists)]

    def chunk(i, carry):
        sl = pl.ds(pl.multiple_of(i * L, L), L)
        streams = [(bi, g) for bi in range(r_ref.shape[0]) for g in range(groups)]
        load = lambda ref: [ref[bi, sl, g * W:(g + 1) * W] for bi, g in streams]
        f32 = lambda xs_: [t.astype(F32) for t in xs_]
        r, k, v, kk, b = (f32(load(ref)) for ref in (r_ref, k_ref, v_ref, kk_ref, b_ref))
        lw = load(lw_ref)
        state = [s_ref[n] for n in range(len(streams))]

        cum = each(cumsum, lw)
        x = each(lambda r_, kk_, c_, lw_: rows(r_ * jnp.exp(c_), kk_ * jnp.exp(c_ - lw_)),
                 r, kk, cum, lw)
        inv_gam = each(lambda c_: jnp.exp(-c_), cum)
        ysm = each(lambda b_, k_, ig: rows(stack(b_ * ig), stack(k_ * ig)), b, k, inv_gam)
        vsm = each(stack, v)
        xs = each(_dot_nt, x, state)
        acs = each(lambda x_, y_: _dot_nt(x_, y_) * causal, x, ysm)
        av = each(lambda a, vs: _dot(rows(a[L:, W:], a[:L, W:]), vs), acs, vsm)
        w = each(lambda xs_, av_: xs_[L:] + av_[:L], xs, av)

        m = each(lambda a: -a[L:, :W], acs)
        p = each(lambda m_: eye + m_, m)
        m = each(lambda m_: _dot(m_, stack(m_)), m)
        for _ in range(4):
            mp = each(lambda m_, p_: _dot(rows(m_, p_), stack(m_)), m, p)
            m = each(lambda mp_: mp_[:L], mp)
            p = each(lambda p_, mp_: p_ + mp_[L:], p, mp)
        p = each(lambda m_, p_: p_ + _dot(p_, stack(m_)), m, p)

        sa = each(lambda p_, w_: -_dot(p_, stack(w_)), p, w)
        y = each(lambda xs_, av_, a, sa_: xs_[:L] + av_[L:] + _dot(a[:L, :W], stack(sa_)),
                 xs, av, acs, sa)
        for (bi, g), y_ in zip(streams, y):
            y_ref[bi, sl, g * W:(g + 1) * W] = y_

        def new_state(s_, sa_, v_, b_, k_, c_):
            c_last = c_[L - 1:L, :]
            to_end = jnp.exp(c_last - c_)
            upd = _dot_tn(rows(sa_, v_), rows(b_ * to_end, k_ * to_end))
            return s_ * jnp.exp(c_last) + upd * bd

        for n, s_new in enumerate(each(new_state, state, sa, v, b, k, cum)):
            s_ref[n] = s_new
        return carry

    lax.fori_loop(0, r_ref.shape[1] // L, chunk, 0)


def _rwkv_scan(r, lw, k, v, kk, b):
    bsz, s, c = r.shape
    lt = 256
    nb = 2
    tok = pl.BlockSpec((nb, lt, c), lambda i, j: (i, j, 0))
    return pl.pallas_call(
        _scan_kernel,
        grid=(bsz // nb, s // lt),
        in_specs=[tok] * 6,
        out_specs=tok,
        out_shape=jax.ShapeDtypeStruct((bsz, s, c), F32),
        scratch_shapes=[pltpu.VMEM((nb * c // SCAN_LANES, SCAN_LANES, SCAN_LANES), F32)],
        compiler_params=_params("parallel", "arbitrary"),
        name="rwkv_scan",
    )(r, lw, k, v, kk, b)


def _rwkv_out_kernel(x_ref, y_ref, r_ref, k_ref, v_ref, g_ref, rk_ref, lnw_ref, lnb_ref,
                     wo_ref, o_ref):
    ones = _head_ones(SCAN_LANES)
    head_sum = lambda t: _head_sums(t, ones)
    y = y_ref[...]
    inv_n = 1.0 / RWKV_HEAD
    d = y - head_sum(y) * inv_n
    var = head_sum(d * d) * inv_n
    yn = d * lax.rsqrt(var + GN_EPS) * lnw_ref[...] + lnb_ref[...]
    f32 = lambda ref: ref[...].astype(F32)
    bonus = head_sum(f32(r_ref) * f32(k_ref) * rk_ref[...]) * f32(v_ref)
    o_ref[...] = x_ref[...] + _dot((yn + bonus) * f32(g_ref), wo_ref[...])


def _rwkv_out(x2d, y, r, k, v, g, r_k, ln_w, ln_b, w_o):
    t, c = x2d.shape
    tm = 1024
    tok = pl.BlockSpec((tm, c), lambda i: (i, 0))
    consts = [r_k, ln_w, ln_b, w_o]
    return pl.pallas_call(
        _rwkv_out_kernel,
        grid=(t // tm,),
        in_specs=[tok] * 6 + [_resident(a.shape) for a in consts],
        out_specs=tok,
        out_shape=jax.ShapeDtypeStruct((t, c), F32),
        compiler_params=_params("parallel"),
        name="rwkv_out",
    )(x2d, y, r, k, v, g, *consts)


def _gelu_tanh(x):
    return 0.5 * x * (1.0 + jnp.tanh(0.7978845608028654 * (x + 0.044715 * (x * x * x))))


def _gmlp_kernel(x_ref, g_ref, wuv_ref, vn_ref, ws_ref, bs_ref, wo_ref, o_ref, gated_ref):
    x = x_ref[...]
    tm = x.shape[0]
    width = wo_ref.shape[0]
    T = GMLP_CHUNK
    D = GMLP_GROUP_DIM
    h = _rms(x, g_ref[...]).astype(BF16)
    u = _gelu_tanh(jnp.dot(h, wuv_ref[:, :width], preferred_element_type=F32))
    v = _gelu_tanh(jnp.dot(h, wuv_ref[:, width:], preferred_element_type=F32))
    v = _rms(v, vn_ref[...]).astype(BF16)
    row = lax.broadcasted_iota(jnp.int32, (T, T), 0)
    col = lax.broadcasted_iota(jnp.int32, (T, T), 1)
    causal = col <= row
    for grp in range(width // D):
        lanes = slice(grp * D, (grp + 1) * D)
        ws = jnp.where(causal, ws_ref[grp], 0.0).astype(BF16)
        bias = bs_ref[:, grp:grp + 1]
        for ch in range(tm // T):
            rows = slice(ch * T, (ch + 1) * T)
            sv = jnp.dot(ws, v[rows, lanes], preferred_element_type=F32) + bias
            gated_ref[rows, lanes] = (u[rows, lanes] * sv).astype(BF16)
    o_ref[...] = x + jnp.dot(gated_ref[...], wo_ref[...], preferred_element_type=F32)


def _gmlp(x2d, g, w_uv, v_norm, w_s, b_s_t, w_o):
    t, c = x2d.shape
    width = w_o.shape[0]
    tm = 512
    consts = [g, w_uv, v_norm, w_s, b_s_t, w_o]
    return pl.pallas_call(
        _gmlp_kernel,
        grid=(t // tm,),
        in_specs=[pl.BlockSpec((tm, c), lambda i: (i, 0))] + [_resident(a.shape) for a in consts],
        out_specs=pl.BlockSpec((tm, c), lambda i: (i, 0)),
        out_shape=jax.ShapeDtypeStruct((t, c), F32),
        scratch_shapes=[pltpu.VMEM((tm, width), BF16)],
        compiler_params=_params("parallel"),
        name="gmlp",
    )(x2d, *consts)


def _pad_cols(w, n):
    return jnp.pad(w, ((0, 0), (0, n - w.shape[1])))


def _pad_rows(w, n):
    return jnp.pad(w, ((0, n - w.shape[0]), (0, 0)))


def _round_up(n, m):
    return -(-n // m) * m


def _rwkv_layer(x3d, mix_g, mu, w_rkv, w0, w1, w2, a0, a1, a2, g1, g2, k_k, k_a, r_k,
                ln_w, ln_b, w_o):
    b, s, c = x3d.shape
    row = lambda p: p.reshape(1, c)
    rd = _round_up(w1.shape[1], LANES)
    ra = _round_up(a1.shape[1], LANES)
    rg = _round_up(g1.shape[1], LANES)
    r, lw, k, v, kk, bb, g = _rwkv_proj(
        x3d, row(mix_g), mu, w_rkv.astype(BF16), row(w0),
        _pad_cols(w1, rd).astype(BF16), _pad_rows(w2, rd).astype(BF16), row(a0),
        _pad_cols(a1, ra).astype(BF16), _pad_rows(a2, ra).astype(BF16),
        _pad_cols(g1, rg).astype(BF16), _pad_rows(g2, rg).astype(BF16),
        row(k_k), row(k_a))
    y = _rwkv_scan(r, lw, k, v, kk, bb)
    flat = lambda t: t.reshape(b * s, c)
    out = _rwkv_out(flat(x3d), flat(y), flat(r), flat(k), flat(v), flat(g),
                    row(r_k), row(ln_w), row(ln_b), w_o.astype(BF16))
    return out.reshape(b, s, c)


def kernel(x, mem, mem_norm, mem_w_kv, ffn1_norm, ffn1_w_in, ffn1_w_out, mix_norm, xattn_norm, xattn_wq, xattn_wo, ffn2_norm, ffn2_w_in, ffn2_w_out, rwkv_mu, rwkv_w_rkv, rwkv_w0, rwkv_w1, rwkv_w2, rwkv_a0, rwkv_a1, rwkv_a2, rwkv_g1, rwkv_g2, rwkv_k_k, rwkv_k_a, rwkv_r_k, rwkv_ln_w, rwkv_ln_b, rwkv_w_o, gmlp_w_uv, gmlp_v_norm, gmlp_w_s, gmlp_b_s, gmlp_w_o, final_norm):
    b, s, c = x.shape
    depth = ffn1_norm.shape[0]
    row = lambda p: p.reshape(1, -1)

    folded = _mem_fold(mem, row(mem_norm), mem_w_kv.astype(BF16), xattn_wq.astype(BF16),
                       xattn_wo.astype(BF16))

    x2d = x.reshape(b * s, c)
    for i in range(depth):
        x2d = _ffn(x2d, row(ffn1_norm[i]), ffn1_w_in[i].astype(BF16), ffn1_w_out[i].astype(BF16))
        j = i // 2
        if i % 2 == 0:
            x3d = _rwkv_layer(
                x2d.reshape(b, s, c), mix_norm[i], rwkv_mu[j], rwkv_w_rkv[j], rwkv_w0[j],
                rwkv_w1[j], rwkv_w2[j], rwkv_a0[j], rwkv_a1[j], rwkv_a2[j], rwkv_g1[j],
                rwkv_g2[j], rwkv_k_k[j], rwkv_k_a[j], rwkv_r_k[j], rwkv_ln_w[j],
                rwkv_ln_b[j], rwkv_w_o[j])
        else:
            x2d = _gmlp(x2d, row(mix_norm[i]), gmlp_w_uv[j].astype(BF16), row(gmlp_v_norm[j]),
                        gmlp_w_s[j], jnp.transpose(gmlp_b_s[j]), gmlp_w_o[j].astype(BF16))
            x3d = x2d.reshape(b, s, c)
        x3d = _xattn(x3d, row(xattn_norm[i]), *folded[i])
        x2d = _ffn(x3d.reshape(b * s, c), row(ffn2_norm[i]), ffn2_w_in[i].astype(BF16),
                   ffn2_w_out[i].astype(BF16),
                   final_g=row(final_norm) if i == depth - 1 else None)
    return x2d.reshape(b, s, c)
```

```python
import functools

import jax
import jax.numpy as jnp
from jax import lax
from jax.experimental import pallas as pl
from jax.experimental.pallas import tpu as pltpu

F32 = jnp.float32
BF16 = jnp.bfloat16

RMS_EPS = 1e-6
GN_EPS = 64e-5
RWKV_HEAD = 64
GMLP_CHUNK = 128
GMLP_GROUP_DIM = 128
XATTN_HEADS = 4

LANES = 128
SCAN_CHUNK = 64
MXU_WIDTH = 256
SCAN_LANES = MXU_WIDTH
VMEM_LIMIT = 52 * 1024 * 1024


def _resident(shape):
    nd = len(shape)
    return pl.BlockSpec(shape, lambda *_: (0,) * nd, pipeline_mode=pl.Buffered(1))


def _params(*sem):
    return pltpu.CompilerParams(dimension_semantics=sem, vmem_limit_bytes=VMEM_LIMIT)


def _rms(x, g):
    return x * lax.rsqrt(jnp.mean(x * x, axis=-1, keepdims=True) + RMS_EPS) * g


def _dot(a, b):
    return jnp.dot(a.astype(BF16), b.astype(BF16), preferred_element_type=F32)


def _dot_nt(a, b):
    return lax.dot_general(a.astype(BF16), b.astype(BF16), (((1,), (1,)), ((), ())),
                           preferred_element_type=F32)


def _dot_tn(a, b):
    return lax.dot_general(a.astype(BF16), b.astype(BF16), (((0,), (0,)), ((), ())),
                           preferred_element_type=F32)


def _split2(x):
    hi = x.astype(BF16)
    lo = (x - hi.astype(F32)).astype(BF16)
    return hi, lo


def _head_ones(width):
    r = lax.broadcasted_iota(jnp.int32, (width, width), 0) // RWKV_HEAD
    c = lax.broadcasted_iota(jnp.int32, (width, width), 1) // RWKV_HEAD
    return (r == c).astype(BF16)


def _head_sums(x, ones):
    w = ones.shape[0]
    hi, lo = _split2(x)
    parts = [jnp.dot(hi[:, i:i + w], ones, preferred_element_type=F32)
             + jnp.dot(lo[:, i:i + w], ones, preferred_element_type=F32)
             for i in range(0, x.shape[1], w)]
    return jnp.concatenate(parts, axis=1)


def _mem_fold_kernel(m_ref, g_ref, wkv_ref, wq_ref, wo_ref, *outs):
    c = m_ref.shape[-1]
    hd = c // XATTN_HEADS
    n_mem = m_ref.shape[1]
    kv = _dot(_rms(m_ref[0], g_ref[...]), wkv_ref[...])
    k = kv[:, :c].astype(BF16)
    v = kv[:, c:].astype(BF16)
    for layer in range(wq_ref.shape[0]):
        qk_ref, vo_ref = outs[2 * layer], outs[2 * layer + 1]
        for i in range(XATTN_HEADS):
            sl = slice(i * hd, (i + 1) * hd)
            qk_ref[0, :, i * n_mem:(i + 1) * n_mem] = _dot_nt(wq_ref[layer, :, sl], k[:, sl]).astype(BF16)
            vo_ref[0, i * n_mem:(i + 1) * n_mem, :] = _dot(v[:, sl], wo_ref[layer, sl, :]).astype(BF16)


def _mem_fold(mem, g, w_kv, wq, wo):
    b, m, c = mem.shape
    depth = wq.shape[0]
    hm = XATTN_HEADS * m
    shapes = [jax.ShapeDtypeStruct((b, c, hm), BF16), jax.ShapeDtypeStruct((b, hm, c), BF16)] * depth
    specs = [pl.BlockSpec((1, c, hm), lambda i: (i, 0, 0)), pl.BlockSpec((1, hm, c), lambda i: (i, 0, 0))] * depth
    outs = pl.pallas_call(
        _mem_fold_kernel,
        grid=(b,),
        in_specs=[pl.BlockSpec((1, m, c), lambda i: (i, 0, 0)), _resident(g.shape),
                  _resident(w_kv.shape), _resident(wq.shape), _resident(wo.shape)],
        out_specs=specs,
        out_shape=shapes,
        compiler_params=_params("parallel"),
        name="mem_fold",
    )(mem, g, w_kv, wq, wo)
    return [(outs[2 * i], outs[2 * i + 1]) for i in range(depth)]


def _ffn_kernel(x_ref, g_ref, win_ref, wout_ref, *rest, f_chunk, final):
    o_ref = rest[-1]
    d_ff = wout_ref.shape[0]
    x = x_ref[...]
    h = _rms(x, g_ref[...]).astype(BF16)
    acc = jnp.zeros_like(x)
    for lo in range(0, d_ff, f_chunk):
        gate = jnp.dot(h, win_ref[:, lo:lo + f_chunk], preferred_element_type=F32)
        up = jnp.dot(h, win_ref[:, d_ff + lo:d_ff + lo + f_chunk], preferred_element_type=F32)
        act = (gate * jax.nn.sigmoid(gate) * up).astype(BF16)
        acc = acc + jnp.dot(act, wout_ref[lo:lo + f_chunk, :], preferred_element_type=F32)
    y = x + 0.5 * acc
    if final:
        y = _rms(y, rest[0][...])
    o_ref[...] = y


def _ffn(x2d, g, w_in, w_out, layer, final_g=None):
    t, c = x2d.shape
    d_ff = w_out.shape[1]
    tm = 1024
    f_chunk = MXU_WIDTH
    assert d_ff % f_chunk == 0
    final = final_g is not None
    layer_block = lambda shape: pl.BlockSpec((None,) + shape, lambda i: (layer, 0, 0),
                                             pipeline_mode=pl.Buffered(1))
    in_specs = [pl.BlockSpec((tm, c), lambda i: (i, 0)), _resident((1, c)),
                layer_block((c, 2 * d_ff)), layer_block((d_ff, c))]
    args = [x2d, g, w_in, w_out]
    if final:
        in_specs.append(_resident((1, c)))
        args.append(final_g)
    return pl.pallas_call(
        functools.partial(_ffn_kernel, f_chunk=f_chunk, final=final),
        grid=(t // tm,),
        in_specs=in_specs,
        out_specs=pl.BlockSpec((tm, c), lambda i: (i, 0)),
        out_shape=jax.ShapeDtypeStruct((t, c), F32),
        compiler_params=_params("parallel"),
        name="ffn_final" if final else "ffn",
    )(*args)


def _xattn_kernel(x_ref, g_ref, qk_ref, vo_ref, o_ref):
    x = x_ref[0]
    hm = qk_ref.shape[2]
    n_mem = hm // XATTN_HEADS
    scale = (x.shape[-1] // XATTN_HEADS) ** -0.5
    s = _dot(_rms(x, g_ref[...]), qk_ref[0]) * scale
    probs = []
    for i in range(XATTN_HEADS):
        si = s[:, i * n_mem:(i + 1) * n_mem]
        p = jnp.exp(si - jnp.max(si, axis=-1, keepdims=True))
        probs.append((p / jnp.sum(p, axis=-1, keepdims=True)).astype(BF16))
    o_ref[0] = x + jnp.dot(jnp.concatenate(probs, axis=1), vo_ref[0], preferred_element_type=F32)


def _xattn(x3d, g, qk, vo):
    b, s, c = x3d.shape
    hm = qk.shape[2]
    tm = 1024
    return pl.pallas_call(
        _xattn_kernel,
        grid=(b, s // tm),
        in_specs=[pl.BlockSpec((1, tm, c), lambda i, j: (i, j, 0)), _resident((1, c)),
                  pl.BlockSpec((1, c, hm), lambda i, j: (i, 0, 0)),
                  pl.BlockSpec((1, hm, c), lambda i, j: (i, 0, 0))],
        out_specs=pl.BlockSpec((1, tm, c), lambda i, j: (i, j, 0)),
        out_shape=jax.ShapeDtypeStruct((b, s, c), F32),
        compiler_params=_params("parallel", "parallel"),
        name="xattn",
    )(x3d, g, qk, vo)


def _rwkv_proj_kernel(x_ref, g_ref, mu_ref, wr_ref, wk_ref, wv_ref, w0_ref, w1_ref, w2_ref,
                      a0_ref, a1_ref, a2_ref, g1_ref, g2_ref, kk_ref, ka_ref,
                      r_out, lw_out, k_out, v_out, kk_out, b_out, g_out, carry_ref):
    @pl.when(pl.program_id(1) == 0)
    def _():
        carry_ref[...] = jnp.zeros_like(carry_ref)

    x = x_ref[0]
    tm = x.shape[0]
    h = _rms(x, g_ref[...])
    row = lax.broadcasted_iota(jnp.int32, h.shape, 0)
    prev = jnp.where(row == 0, carry_ref[0:1, :], pltpu.roll(h, 1, axis=0))
    carry_ref[0:1, :] = h[tm - 1:tm, :]
    dx = prev - h

    def mix(i):
        return h + dx * mu_ref[i:i + 1, :]

    dw = jnp.tanh(_dot(mix(1), w1_ref[...]))
    da = _dot(mix(4), a1_ref[...])
    dg = jax.nn.sigmoid(_dot(mix(5), g1_ref[...]))
    k = _dot(mix(2), wk_ref[...])
    z = w0_ref[...] + _dot(dw, w2_ref[...])
    a = jax.nn.sigmoid(a0_ref[...] + _dot(da, a2_ref[...]))
    g_out[0] = _dot(dg, g2_ref[...]).astype(g_out.dtype)
    r = _dot(mix(0), wr_ref[...])
    kkr = k * kk_ref[...]
    kk = kkr * lax.rsqrt(jnp.maximum(_head_sums(kkr * kkr, _head_ones(SCAN_LANES)), 1e-24))
    v = _dot(mix(3), wv_ref[...])
    u = -z
    softplus = jnp.maximum(u, 0.0) + jnp.log(1.0 + jnp.exp(-jnp.abs(u)))
    lw_out[0] = -jnp.exp(-softplus - 0.5)
    r_out[0] = r.astype(r_out.dtype)
    v_out[0] = v.astype(v_out.dtype)
    k_out[0] = (k * (1.0 + (a - 1.0) * ka_ref[...])).astype(k_out.dtype)
    kk_out[0] = kk.astype(kk_out.dtype)
    b_out[0] = (kk * a).astype(b_out.dtype)


def _rwkv_proj(x3d, g, mu, w_rkv, w0, w1, w2, a0, a1, a2, g1, g2, k_k, k_a):
    b, s, c = x3d.shape
    tm = 512
    tok = pl.BlockSpec((1, tm, c), lambda i, j: (i, j, 0))
    rkv = [pl.BlockSpec((c, c), lambda i, j, n=n: (0, n), pipeline_mode=pl.Buffered(1))
           for n in range(3)]
    consts = [w0, w1, w2, a0, a1, a2, g1, g2, k_k, k_a]
    out = [jax.ShapeDtypeStruct((b, s, c), F32 if i == 1 else BF16) for i in range(7)]
    return pl.pallas_call(
        _rwkv_proj_kernel,
        grid=(b, s // tm),
        in_specs=([tok, _resident(g.shape), _resident(mu.shape)] + rkv
                  + [_resident(a.shape) for a in consts]),
        out_specs=[tok] * 7,
        out_shape=out,
        scratch_shapes=[pltpu.VMEM((8, c), F32)],
        compiler_params=_params("parallel", "arbitrary"),
        name="rwkv_proj",
    )(x3d, g, mu, w_rkv, w_rkv, w_rkv, *consts)


def _scan_kernel(r_ref, lw_ref, k_ref, v_ref, kk_ref, b_ref, y_ref, s_ref):
    L = SCAN_CHUNK
    W = SCAN_LANES
    G = W // RWKV_HEAD
    groups = r_ref.shape[2] // W

    @pl.when(pl.program_id(1) == 0)
    def _():
        s_ref[...] = jnp.zeros_like(s_ref)

    rr = lax.broadcasted_iota(jnp.int32, (W, W), 0)
    cc = lax.broadcasted_iota(jnp.int32, (W, W), 1)
    head_bd = rr // RWKV_HEAD == cc // RWKV_HEAD
    bd = head_bd.astype(F32)
    bd16 = head_bd.astype(BF16)
    ar = lax.broadcasted_iota(jnp.int32, (2 * L, 2 * W), 0)
    ac = lax.broadcasted_iota(jnp.int32, (2 * L, 2 * W), 1) % L
    causal = (ac <= jnp.where(ar < L, ar, ar - L - 1)).astype(F32)
    tr = lax.broadcasted_iota(jnp.int32, (L, W), 0)
    tc = lax.broadcasted_iota(jnp.int32, (L, W), 1) % L
    eye = (tc == tr).astype(F32)
    t64r = lax.broadcasted_iota(jnp.int32, (L, L), 0)
    t64c = lax.broadcasted_iota(jnp.int32, (L, L), 1)
    tri = (t64c <= t64r).astype(BF16)

    def stack(x):
        return jnp.concatenate([x.astype(BF16)] * G, axis=0) * bd16

    def rows(*parts):
        return jnp.concatenate(parts, axis=0)

    def cumsum(lw):
        hi, lo = _split2(lw)
        return (jnp.dot(tri, hi, preferred_element_type=F32)
                + jnp.dot(tri, lo, preferred_element_type=F32))

    def each(fn, *lists):
        return [fn(*args) for args in zip(*lists)]

    def chunk(i, carry):
        sl = pl.ds(pl.multiple_of(i * L, L), L)
        streams = [(bi, g) for bi in range(r_ref.shape[0]) for g in range(groups)]
        load = lambda ref: [ref[bi, sl, g * W:(g + 1) * W] for bi, g in streams]
        f32 = lambda xs_: [t.astype(F32) for t in xs_]
        r, k, v, kk, b = (f32(load(ref)) for ref in (r_ref, k_ref, v_ref, kk_ref, b_ref))
        lw = load(lw_ref)
        state = [s_ref[n] for n in range(len(streams))]

        cum = each(cumsum, lw)
        x = each(lambda r_, kk_, c_, lw_: rows(r_ * jnp.exp(c_), kk_ * jnp.exp(c_ - lw_)),
                 r, kk, cum, lw)
        inv_gam = each(lambda c_: jnp.exp(-c_), cum)
        ysm = each(lambda b_, k_, ig: rows(stack(b_ * ig), stack(k_ * ig)), b, k, inv_gam)
        vsm = each(stack, v)
        xs = each(_dot_nt, x, state)
        acs = each(lambda x_, y_: _dot_nt(x_, y_) * causal, x, ysm)
        av = each(lambda a, vs: _dot(rows(a[L:, W:], a[:L, W:]), vs), acs, vsm)
        w = each(lambda xs_, av_: xs_[L:] + av_[:L], xs, av)

        m = each(lambda a: -a[L:, :W], acs)
        p = each(lambda m_: eye + m_, m)
        m = each(lambda m_: _dot(m_, stack(m_)), m)
        for _ in range(4):
            mp = each(lambda m_, p_: _dot(rows(m_, p_), stack(m_)), m, p)
            m = each(lambda mp_: mp_[:L], mp)
            p = each(lambda p_, mp_: p_ + mp_[L:], p, mp)
        p = each(lambda m_, p_: p_ + _dot(p_, stack(m_)), m, p)

        sa = each(lambda p_, w_: -_dot(p_, stack(w_)), p, w)
        y = each(lambda xs_, av_, a, sa_: xs_[:L] + av_[L:] + _dot(a[:L, :W], stack(sa_)),
                 xs, av, acs, sa)
        for (bi, g), y_ in zip(streams, y):
            y_ref[bi, sl, g * W:(g + 1) * W] = y_

        def new_state(s_, sa_, v_, b_, k_, c_):
            c_last = c_[L - 1:L, :]
            to_end = jnp.exp(c_last - c_)
            upd = _dot_tn(rows(sa_, v_), rows(b_ * to_end, k_ * to_end))
            return s_ * jnp.exp(c_last) + upd * bd

        for n, s_new in enumerate(each(new_state, state, sa, v, b, k, cum)):
            s_ref[n] = s_new
        return carry

    lax.fori_loop(0, r_ref.shape[1] // L, chunk, 0)


def _rwkv_scan(r, lw, k, v, kk, b):
    bsz, s, c = r.shape
    lt = 128
    nb = 4
    tok = pl.BlockSpec((nb, lt, c), lambda i, j: (i, j, 0))
    return pl.pallas_call(
        _scan_kernel,
        grid=(bsz // nb, s // lt),
        in_specs=[tok] * 6,
        out_specs=tok,
        out_shape=jax.ShapeDtypeStruct((bsz, s, c), F32),
        scratch_shapes=[pltpu.VMEM((nb * c // SCAN_LANES, SCAN_LANES, SCAN_LANES), F32)],
        compiler_params=_params("parallel", "arbitrary"),
        name="rwkv_scan",
    )(r, lw, k, v, kk, b)


def _rwkv_out_kernel(x_ref, y_ref, r_ref, k_ref, v_ref, g_ref, rk_ref, lnw_ref, lnb_ref,
                     wo_ref, o_ref):
    ones = _head_ones(SCAN_LANES)
    head_sum = lambda t: _head_sums(t, ones)
    y = y_ref[...]
    inv_n = 1.0 / RWKV_HEAD
    d = y - head_sum(y) * inv_n
    var = head_sum(d * d) * inv_n
    yn = d * lax.rsqrt(var + GN_EPS) * lnw_ref[...] + lnb_ref[...]
    f32 = lambda ref: ref[...].astype(F32)
    bonus = head_sum(f32(r_ref) * f32(k_ref) * rk_ref[...]) * f32(v_ref)
    o_ref[...] = x_ref[...] + _dot((yn + bonus) * f32(g_ref), wo_ref[...])


def _rwkv_out(x2d, y, r, k, v, g, r_k, ln_w, ln_b, w_o):
    t, c = x2d.shape
    tm = 1024
    tok = pl.BlockSpec((tm, c), lambda i: (i, 0))
    consts = [r_k, ln_w, ln_b, w_o]
    return pl.pallas_call(
        _rwkv_out_kernel,
        grid=(t // tm,),
        in_specs=[tok] * 6 + [_resident(a.shape) for a in consts],
        out_specs=tok,
        out_shape=jax.ShapeDtypeStruct((t, c), F32),
        compiler_params=_params("parallel"),
        name="rwkv_out",
    )(x2d, y, r, k, v, g, *consts)


def _gelu_tanh(x):
    return 0.5 * x * (1.0 + jnp.tanh(0.7978845608028654 * (x + 0.044715 * (x * x * x))))


def _gmlp_kernel(x_ref, g_ref, wuv_ref, vn_ref, ws_ref, bs_ref, wo_ref, o_ref, gated_ref):
    x = x_ref[...]
    tm = x.shape[0]
    width = wo_ref.shape[0]
    T = GMLP_CHUNK
    D = GMLP_GROUP_DIM
    h = _rms(x, g_ref[...]).astype(BF16)
    u = _gelu_tanh(jnp.dot(h, wuv_ref[:, :width], preferred_element_type=F32))
    v = _gelu_tanh(jnp.dot(h, wuv_ref[:, width:], preferred_element_type=F32))
    v = _rms(v, vn_ref[...]).astype(BF16)
    row = lax.broadcasted_iota(jnp.int32, (T, T), 0)
    col = lax.broadcasted_iota(jnp.int32, (T, T), 1)
    causal = col <= row
    for grp in range(width // D):
        lanes = slice(grp * D, (grp + 1) * D)
        ws = jnp.where(causal, ws_ref[grp], 0.0).astype(BF16)
        bias = bs_ref[:, grp:grp + 1]
        nch = tm // T
        vcat = jnp.concatenate([v[ch * T:(ch + 1) * T, lanes] for ch in range(nch)], axis=1)
        sv = jnp.dot(ws, vcat, preferred_element_type=F32) + bias
        for ch in range(nch):
            rows = slice(ch * T, (ch + 1) * T)
            gated_ref[rows, lanes] = (u[rows, lanes] * sv[:, ch * D:(ch + 1) * D]).astype(BF16)
    o_ref[...] = x + jnp.dot(gated_ref[...], wo_ref[...], preferred_element_type=F32)


def _gmlp(x2d, g, w_uv, v_norm, w_s, b_s_t, w_o):
    t, c = x2d.shape
    width = w_o.shape[0]
    tm = 512
    consts = [g, w_uv, v_norm, w_s, b_s_t, w_o]
    return pl.pallas_call(
        _gmlp_kernel,
        grid=(t // tm,),
        in_specs=[pl.BlockSpec((tm, c), lambda i: (i, 0))] + [_resident(a.shape) for a in consts],
        out_specs=pl.BlockSpec((tm, c), lambda i: (i, 0)),
        out_shape=jax.ShapeDtypeStruct((t, c), F32),
        scratch_shapes=[pltpu.VMEM((tm, width), BF16)],
        compiler_params=_params("parallel"),
        name="gmlp",
    )(x2d, *consts)


def _pad_cols(w, n):
    return jnp.pad(w, ((0, 0), (0, n - w.shape[1])))


def _pad_rows(w, n):
    return jnp.pad(w, ((0, n - w.shape[0]), (0, 0)))


def _round_up(n, m):
    return -(-n // m) * m


def _rwkv_layer(x3d, mix_g, mu, w_rkv, w0, w1, w2, a0, a1, a2, g1, g2, k_k, k_a, r_k,
                ln_w, ln_b, w_o):
    b, s, c = x3d.shape
    row = lambda p: p.reshape(1, c)
    rd = _round_up(w1.shape[1], LANES)
    ra = _round_up(a1.shape[1], LANES)
    rg = _round_up(g1.shape[1], LANES)
    r, lw, k, v, kk, bb, g = _rwkv_proj(
        x3d, row(mix_g), mu, w_rkv.astype(BF16), row(w0),
        _pad_cols(w1, rd).astype(BF16), _pad_rows(w2, rd).astype(BF16), row(a0),
        _pad_cols(a1, ra).astype(BF16), _pad_rows(a2, ra).astype(BF16),
        _pad_cols(g1, rg).astype(BF16), _pad_rows(g2, rg).astype(BF16),
        row(k_k), row(k_a))
    y = _rwkv_scan(r, lw, k, v, kk, bb)
    flat = lambda t: t.reshape(b * s, c)
    out = _rwkv_out(flat(x3d), flat(y), flat(r), flat(k), flat(v), flat(g),
                    row(r_k), row(ln_w), row(ln_b), w_o.astype(BF16))
    return out.reshape(b, s, c)


def kernel(x, mem, mem_norm, mem_w_kv, ffn1_norm, ffn1_w_in, ffn1_w_out, mix_norm, xattn_norm, xattn_wq, xattn_wo, ffn2_norm, ffn2_w_in, ffn2_w_out, rwkv_mu, rwkv_w_rkv, rwkv_w0, rwkv_w1, rwkv_w2, rwkv_a0, rwkv_a1, rwkv_a2, rwkv_g1, rwkv_g2, rwkv_k_k, rwkv_k_a, rwkv_r_k, rwkv_ln_w, rwkv_ln_b, rwkv_w_o, gmlp_w_uv, gmlp_v_norm, gmlp_w_s, gmlp_b_s, gmlp_w_o, final_norm):
    b, s, c = x.shape
    depth = ffn1_norm.shape[0]
    row = lambda p: p.reshape(1, -1)

    folded = _mem_fold(mem, row(mem_norm), mem_w_kv.astype(BF16), xattn_wq.astype(BF16),
                       xattn_wo.astype(BF16))

    w_in1, w_out1 = ffn1_w_in.astype(BF16), ffn1_w_out.astype(BF16)
    w_in2, w_out2 = ffn2_w_in.astype(BF16), ffn2_w_out.astype(BF16)
    x2d = x.reshape(b * s, c)
    for i in range(depth):
        x2d = _ffn(x2d, row(ffn1_norm[i]), w_in1, w_out1, i)
        j = i // 2
        if i % 2 == 0:
            x3d = _rwkv_layer(
                x2d.reshape(b, s, c), mix_norm[i], rwkv_mu[j], rwkv_w_rkv[j], rwkv_w0[j],
                rwkv_w1[j], rwkv_w2[j], rwkv_a0[j], rwkv_a1[j], rwkv_a2[j], rwkv_g1[j],
                rwkv_g2[j], rwkv_k_k[j], rwkv_k_a[j], rwkv_r_k[j], rwkv_ln_w[j],
                rwkv_ln_b[j], rwkv_w_o[j])
        else:
            x2d = _gmlp(x2d, row(mix_norm[i]), gmlp_w_uv[j].astype(BF16), row(gmlp_v_norm[j]),
                        gmlp_w_s[j], jnp.transpose(gmlp_b_s[j]), gmlp_w_o[j].astype(BF16))
            x3d = x2d.reshape(b, s, c)
        x3d = _xattn(x3d, row(xattn_norm[i]), *folded[i])
        x2d = _ffn(x3d.reshape(b * s, c), row(ffn2_norm[i]), w_in2, w_out2, i,
                   final_g=row(final_norm) if i == depth - 1 else None)
    return x2d.reshape(b, s, c)
```

```python
import functools

import jax
import jax.numpy as jnp
from jax import lax
from jax.experimental import pallas as pl
from jax.experimental.pallas import tpu as pltpu

F32 = jnp.float32
BF16 = jnp.bfloat16

RMS_EPS = 1e-6
GN_EPS = 64e-5
RWKV_HEAD = 64
GMLP_CHUNK = 128
GMLP_GROUP_DIM = 128
XATTN_HEADS = 4

LANES = 128
SCAN_CHUNK = 64
MXU_WIDTH = 256
SCAN_LANES = MXU_WIDTH
VMEM_LIMIT = 52 * 1024 * 1024


def _resident(shape):
    nd = len(shape)
    return pl.BlockSpec(shape, lambda *_: (0,) * nd, pipeline_mode=pl.Buffered(1))


def _params(*sem):
    return pltpu.CompilerParams(dimension_semantics=sem, vmem_limit_bytes=VMEM_LIMIT)


def _rms(x, g):
    return x * lax.rsqrt(jnp.mean(x * x, axis=-1, keepdims=True) + RMS_EPS) * g


def _dot(a, b):
    return jnp.dot(a.astype(BF16), b.astype(BF16), preferred_element_type=F32)


def _dot_nt(a, b):
    return lax.dot_general(a.astype(BF16), b.astype(BF16), (((1,), (1,)), ((), ())),
                           preferred_element_type=F32)


def _dot_tn(a, b):
    return lax.dot_general(a.astype(BF16), b.astype(BF16), (((0,), (0,)), ((), ())),
                           preferred_element_type=F32)


def _split2(x):
    hi = x.astype(BF16)
    lo = (x - hi.astype(F32)).astype(BF16)
    return hi, lo


def _head_ones(width):
    r = lax.broadcasted_iota(jnp.int32, (width, width), 0) // RWKV_HEAD
    c = lax.broadcasted_iota(jnp.int32, (width, width), 1) // RWKV_HEAD
    return (r == c).astype(BF16)


def _head_sums(x, ones, split=True):
    w = ones.shape[0]
    terms = _split2(x) if split else (x.astype(BF16),)
    parts = [sum(jnp.dot(t[:, i:i + w], ones, preferred_element_type=F32) for t in terms)
             for i in range(0, x.shape[1], w)]
    return jnp.concatenate(parts, axis=1)


def _mem_fold_kernel(m_ref, g_ref, wkv_ref, wq_ref, wo_ref, *outs):
    c = m_ref.shape[-1]
    hd = c // XATTN_HEADS
    n_mem = m_ref.shape[1]
    kv = _dot(_rms(m_ref[0], g_ref[...]), wkv_ref[...])
    k = kv[:, :c].astype(BF16)
    v = kv[:, c:].astype(BF16)
    for layer in range(wq_ref.shape[0]):
        qk_ref, vo_ref = outs[2 * layer], outs[2 * layer + 1]
        for i in range(XATTN_HEADS):
            sl = slice(i * hd, (i + 1) * hd)
            qk_ref[0, :, i * n_mem:(i + 1) * n_mem] = _dot_nt(wq_ref[layer, :, sl], k[:, sl]).astype(BF16)
            vo_ref[0, i * n_mem:(i + 1) * n_mem, :] = _dot(v[:, sl], wo_ref[layer, sl, :]).astype(BF16)


def _mem_fold(mem, g, w_kv, wq, wo):
    b, m, c = mem.shape
    depth = wq.shape[0]
    hm = XATTN_HEADS * m
    shapes = [jax.ShapeDtypeStruct((b, c, hm), BF16), jax.ShapeDtypeStruct((b, hm, c), BF16)] * depth
    specs = [pl.BlockSpec((1, c, hm), lambda i: (i, 0, 0)), pl.BlockSpec((1, hm, c), lambda i: (i, 0, 0))] * depth
    outs = pl.pallas_call(
        _mem_fold_kernel,
        grid=(b,),
        in_specs=[pl.BlockSpec((1, m, c), lambda i: (i, 0, 0)), _resident(g.shape),
                  _resident(w_kv.shape), _resident(wq.shape), _resident(wo.shape)],
        out_specs=specs,
        out_shape=shapes,
        compiler_params=_params("parallel"),
        name="mem_fold",
    )(mem, g, w_kv, wq, wo)
    return [(outs[2 * i], outs[2 * i + 1]) for i in range(depth)]


def _ffn_kernel(x_ref, g_ref, win_ref, wout_ref, *rest, f_chunk, final):
    o_ref = rest[-1]
    d_ff = wout_ref.shape[0]
    x = x_ref[...]
    h = _rms(x, g_ref[...]).astype(BF16)
    acc = jnp.zeros_like(x)
    for lo in range(0, d_ff, f_chunk):
        gate = jnp.dot(h, win_ref[:, lo:lo + f_chunk], preferred_element_type=F32)
        up = jnp.dot(h, win_ref[:, d_ff + lo:d_ff + lo + f_chunk], preferred_element_type=F32)
        act = (gate * jax.nn.sigmoid(gate) * up).astype(BF16)
        acc = acc + jnp.dot(act, wout_ref[lo:lo + f_chunk, :], preferred_element_type=F32)
    y = x + 0.5 * acc
    if final:
        y = _rms(y, rest[0][...])
    o_ref[...] = y


def _ffn(x2d, g, w_in, w_out, layer, final_g=None):
    t, c = x2d.shape
    d_ff = w_out.shape[1]
    tm = 1024
    f_chunk = MXU_WIDTH
    assert d_ff % f_chunk == 0
    final = final_g is not None
    layer_block = lambda shape: pl.BlockSpec((None,) + shape, lambda i: (layer, 0, 0),
                                             pipeline_mode=pl.Buffered(1))
    in_specs = [pl.BlockSpec((tm, c), lambda i: (i, 0)), _resident((1, c)),
                layer_block((c, 2 * d_ff)), layer_block((d_ff, c))]
    args = [x2d, g, w_in, w_out]
    if final:
        in_specs.append(_resident((1, c)))
        args.append(final_g)
    return pl.pallas_call(
        functools.partial(_ffn_kernel, f_chunk=f_chunk, final=final),
        grid=(t // tm,),
        in_specs=in_specs,
        out_specs=pl.BlockSpec((tm, c), lambda i: (i, 0)),
        out_shape=jax.ShapeDtypeStruct((t, c), F32),
        compiler_params=_params("parallel"),
        name="ffn_final" if final else "ffn",
    )(*args)


def _xattn_kernel(x_ref, g_ref, qk_ref, vo_ref, o_ref):
    x = x_ref[0]
    hm = qk_ref.shape[2]
    n_mem = hm // XATTN_HEADS
    scale = (x.shape[-1] // XATTN_HEADS) ** -0.5
    s = _dot(_rms(x, g_ref[...]), qk_ref[0]) * scale
    probs = []
    for i in range(XATTN_HEADS):
        si = s[:, i * n_mem:(i + 1) * n_mem]
        p = jnp.exp(si - jnp.max(si, axis=-1, keepdims=True))
        probs.append((p / jnp.sum(p, axis=-1, keepdims=True)).astype(BF16))
    o_ref[0] = x + jnp.dot(jnp.concatenate(probs, axis=1), vo_ref[0], preferred_element_type=F32)


def _xattn(x3d, g, qk, vo):
    b, s, c = x3d.shape
    hm = qk.shape[2]
    tm = 1024
    return pl.pallas_call(
        _xattn_kernel,
        grid=(b, s // tm),
        in_specs=[pl.BlockSpec((1, tm, c), lambda i, j: (i, j, 0)), _resident((1, c)),
                  pl.BlockSpec((1, c, hm), lambda i, j: (i, 0, 0)),
                  pl.BlockSpec((1, hm, c), lambda i, j: (i, 0, 0))],
        out_specs=pl.BlockSpec((1, tm, c), lambda i, j: (i, j, 0)),
        out_shape=jax.ShapeDtypeStruct((b, s, c), F32),
        compiler_params=_params("parallel", "parallel"),
        name="xattn",
    )(x3d, g, qk, vo)


def _rwkv_proj_kernel(x_ref, g_ref, mu_ref, wr_ref, wk_ref, wv_ref, w0_ref, w1_ref, w2_ref,
                      a0_ref, a1_ref, a2_ref, g1_ref, g2_ref, kk_ref, ka_ref,
                      r_out, lw_out, k_out, v_out, kk_out, b_out, g_out, carry_ref):
    @pl.when(pl.program_id(1) == 0)
    def _():
        carry_ref[...] = jnp.zeros_like(carry_ref)

    x = x_ref[0]
    tm = x.shape[0]
    h = _rms(x, g_ref[...])
    row = lax.broadcasted_iota(jnp.int32, h.shape, 0)
    prev = jnp.where(row == 0, carry_ref[0:1, :], pltpu.roll(h, 1, axis=0))
    carry_ref[0:1, :] = h[tm - 1:tm, :]
    dx = prev - h

    def mix(i):
        return h + dx * mu_ref[i:i + 1, :]

    dw = jnp.tanh(_dot(mix(1), w1_ref[...]))
    da = _dot(mix(4), a1_ref[...])
    dg = jax.nn.sigmoid(_dot(mix(5), g1_ref[...]))
    k = _dot(mix(2), wk_ref[...])
    z = w0_ref[...] + _dot(dw, w2_ref[...])
    a = jax.nn.sigmoid(a0_ref[...] + _dot(da, a2_ref[...]))
    g_out[0] = _dot(dg, g2_ref[...]).astype(g_out.dtype)
    r = _dot(mix(0), wr_ref[...])
    kkr = k * kk_ref[...]
    kk = kkr * lax.rsqrt(jnp.maximum(_head_sums(kkr * kkr, _head_ones(SCAN_LANES), split=False), 1e-24))
    v = _dot(mix(3), wv_ref[...])
    u = -z
    softplus = jnp.maximum(u, 0.0) + jnp.log(1.0 + jnp.exp(-jnp.abs(u)))
    lw_out[0] = -jnp.exp(-softplus - 0.5)
    r_out[0] = r.astype(r_out.dtype)
    v_out[0] = v.astype(v_out.dtype)
    k_out[0] = (k * (1.0 + (a - 1.0) * ka_ref[...])).astype(k_out.dtype)
    kk_out[0] = kk.astype(kk_out.dtype)
    b_out[0] = (kk * a).astype(b_out.dtype)


def _rwkv_proj(x3d, g, mu, w_rkv, w0, w1, w2, a0, a1, a2, g1, g2, k_k, k_a):
    b, s, c = x3d.shape
    tm = 512
    tok = pl.BlockSpec((1, tm, c), lambda i, j: (i, j, 0))
    rkv = [pl.BlockSpec((c, c), lambda i, j, n=n: (0, n), pipeline_mode=pl.Buffered(1))
           for n in range(3)]
    consts = [w0, w1, w2, a0, a1, a2, g1, g2, k_k, k_a]
    out = [jax.ShapeDtypeStruct((b, s, c), F32 if i == 1 else BF16) for i in range(7)]
    return pl.pallas_call(
        _rwkv_proj_kernel,
        grid=(b, s // tm),
        in_specs=([tok, _resident(g.shape), _resident(mu.shape)] + rkv
                  + [_resident(a.shape) for a in consts]),
        out_specs=[tok] * 7,
        out_shape=out,
        scratch_shapes=[pltpu.VMEM((8, c), F32)],
        compiler_params=_params("parallel", "arbitrary"),
        name="rwkv_proj",
    )(x3d, g, mu, w_rkv, w_rkv, w_rkv, *consts)


def _scan_kernel(r_ref, lw_ref, k_ref, v_ref, kk_ref, b_ref, y_ref, s_ref):
    L = SCAN_CHUNK
    W = SCAN_LANES
    G = W // RWKV_HEAD
    groups = r_ref.shape[2] // W

    @pl.when(pl.program_id(1) == 0)
    def _():
        s_ref[...] = jnp.zeros_like(s_ref)

    rr = lax.broadcasted_iota(jnp.int32, (W, W), 0)
    cc = lax.broadcasted_iota(jnp.int32, (W, W), 1)
    head_bd = rr // RWKV_HEAD == cc // RWKV_HEAD
    bd = head_bd.astype(F32)
    bd16 = head_bd.astype(BF16)
    ar = lax.broadcasted_iota(jnp.int32, (2 * L, 2 * W), 0)
    ac = lax.broadcasted_iota(jnp.int32, (2 * L, 2 * W), 1) % L
    causal = (ac <= jnp.where(ar < L, ar, ar - L - 1)).astype(F32)
    tr = lax.broadcasted_iota(jnp.int32, (L, W), 0)
    tc = lax.broadcasted_iota(jnp.int32, (L, W), 1) % L
    eye = (tc == tr).astype(F32)
    t64r = lax.broadcasted_iota(jnp.int32, (L, L), 0)
    t64c = lax.broadcasted_iota(jnp.int32, (L, L), 1)
    tri = (t64c <= t64r).astype(BF16)

    def stack(x):
        return jnp.concatenate([x.astype(BF16)] * G, axis=0) * bd16

    def rows(*parts):
        return jnp.concatenate(parts, axis=0)

    def cumsum(lw):
        hi, lo = _split2(lw)
        return (jnp.dot(tri, hi, preferred_element_type=F32)
                + jnp.dot(tri, lo, preferred_element_type=F32))

    def each(fn, *lists):
        return [fn(*args) for args in zip(*lists)]

    def chunk(i, carry):
        sl = pl.ds(pl.multiple_of(i * L, L), L)
        streams = [(bi, g) for bi in range(r_ref.shape[0]) for g in range(groups)]
        load = lambda ref: [ref[bi, sl, g * W:(g + 1) * W] for bi, g in streams]
        f32 = lambda xs_: [t.astype(F32) for t in xs_]
        r, k, v, kk, b = (f32(load(ref)) for ref in (r_ref, k_ref, v_ref, kk_ref, b_ref))
        lw = load(lw_ref)
        state = [s_ref[n] for n in range(len(streams))]

        cum = each(cumsum, lw)
        x = each(lambda r_, kk_, c_, lw_: rows(r_ * jnp.exp(c_), kk_ * jnp.exp(c_ - lw_)),
                 r, kk, cum, lw)
        inv_gam = each(lambda c_: jnp.exp(-c_), cum)
        ysm = each(lambda b_, k_, ig: rows(stack(b_ * ig), stack(k_ * ig)), b, k, inv_gam)
        vsm = each(stack, v)
        xs = each(_dot_nt, x, state)
        acs = each(lambda x_, y_: _dot_nt(x_, y_) * causal, x, ysm)
        av = each(lambda a, vs: _dot(rows(a[L:, W:], a[:L, W:]), vs), acs, vsm)
        w = each(lambda xs_, av_: xs_[L:] + av_[:L], xs, av)

        m = each(lambda a: -a[L:, :W], acs)
        p = each(lambda m_: eye + m_, m)
        m = each(lambda m_: _dot(m_, stack(m_)), m)
        for _ in range(4):
            mp = each(lambda m_, p_: _dot(rows(m_, p_), stack(m_)), m, p)
            m = each(lambda mp_: mp_[:L], mp)
            p = each(lambda p_, mp_: p_ + mp_[L:], p, mp)
        p = each(lambda m_, p_: p_ + _dot(p_, stack(m_)), m, p)

        sa = each(lambda p_, w_: -_dot(p_, stack(w_)), p, w)
        y = each(lambda xs_, av_, a, sa_: xs_[:L] + av_[L:] + _dot(a[:L, :W], stack(sa_)),
                 xs, av, acs, sa)
        for (bi, g), y_ in zip(streams, y):
            y_ref[bi, sl, g * W:(g + 1) * W] = y_

        def new_state(s_, sa_, v_, b_, k_, c_):
            c_last = c_[L - 1:L, :]
            to_end = jnp.exp(c_last - c_)
            upd = _dot_tn(rows(sa_, v_), rows(b_ * to_end, k_ * to_end))
            return s_ * jnp.exp(c_last) + upd * bd

        for n, s_new in enumerate(each(new_state, state, sa, v, b, k, cum)):
            s_ref[n] = s_new
        return carry

    lax.fori_loop(0, r_ref.shape[1] // L, chunk, 0)


def _rwkv_scan(r, lw, k, v, kk, b):
    bsz, s, c = r.shape
    lt = 256
    nb = 4
    tok = pl.BlockSpec((nb, lt, c), lambda i, j: (i, j, 0))
    return pl.pallas_call(
        _scan_kernel,
        grid=(bsz // nb, s // lt),
        in_specs=[tok] * 6,
        out_specs=tok,
        out_shape=jax.ShapeDtypeStruct((bsz, s, c), F32),
        scratch_shapes=[pltpu.VMEM((nb * c // SCAN_LANES, SCAN_LANES, SCAN_LANES), F32)],
        compiler_params=_params("parallel", "arbitrary"),
        name="rwkv_scan",
    )(r, lw, k, v, kk, b)


def _rwkv_out_kernel(x_ref, y_ref, r_ref, k_ref, v_ref, g_ref, rk_ref, lnw_ref, lnb_ref,
                     wo_ref, o_ref):
    ones = _head_ones(SCAN_LANES)
    y = y_ref[...]
    inv_n = 1.0 / RWKV_HEAD
    d = y - _head_sums(y, ones) * inv_n
    var = _head_sums(d * d, ones, split=False) * inv_n
    yn = d * lax.rsqrt(var + GN_EPS) * lnw_ref[...] + lnb_ref[...]
    f32 = lambda ref: ref[...].astype(F32)
    bonus = _head_sums(f32(r_ref) * f32(k_ref) * rk_ref[...], ones, split=False) * f32(v_ref)
    o_ref[...] = x_ref[...] + _dot((yn + bonus) * f32(g_ref), wo_ref[...])


def _rwkv_out(x2d, y, r, k, v, g, r_k, ln_w, ln_b, w_o):
    t, c = x2d.shape
    tm = 1024
    tok = pl.BlockSpec((tm, c), lambda i: (i, 0))
    consts = [r_k, ln_w, ln_b, w_o]
    return pl.pallas_call(
        _rwkv_out_kernel,
        grid=(t // tm,),
        in_specs=[tok] * 6 + [_resident(a.shape) for a in consts],
        out_specs=tok,
        out_shape=jax.ShapeDtypeStruct((t, c), F32),
        compiler_params=_params("parallel"),
        name="rwkv_out",
    )(x2d, y, r, k, v, g, *consts)


def _gelu_tanh(x):
    return 0.5 * x * (1.0 + jnp.tanh(0.7978845608028654 * (x + 0.044715 * (x * x * x))))


def _gmlp_kernel(x_ref, g_ref, wuv_ref, vn_ref, ws_ref, bs_ref, wo_ref, o_ref, gated_ref):
    tm = x_ref.shape[0]
    width = wo_ref.shape[0]
    T = GMLP_CHUNK
    D = GMLP_GROUP_DIM
    row = lax.broadcasted_iota(jnp.int32, (T, T), 0)
    col = lax.broadcasted_iota(jnp.int32, (T, T), 1)
    causal = col <= row
    sub = tm // 2
    halves = [slice(r0, r0 + sub) for r0 in range(0, tm, sub)]
    nch = sub // T
    x = [x_ref[rs, :] for rs in halves]
    h = [_rms(x_, g_ref[...]).astype(BF16) for x_ in x]
    u = [_gelu_tanh(jnp.dot(h_, wuv_ref[:, :width], preferred_element_type=F32)) for h_ in h]
    v = [_gelu_tanh(jnp.dot(h_, wuv_ref[:, width:], preferred_element_type=F32)) for h_ in h]
    v = [_rms(v_, vn_ref[...]).astype(BF16) for v_ in v]
    for grp in range(width // D):
        lanes = slice(grp * D, (grp + 1) * D)
        ws = jnp.where(causal, ws_ref[grp], 0.0).astype(BF16)
        bias = bs_ref[:, grp:grp + 1]
        vcat = jnp.concatenate([v_[ch * T:(ch + 1) * T, lanes] for v_ in v for ch in range(nch)],
                               axis=1)
        sv = jnp.dot(ws, vcat, preferred_element_type=F32) + bias
        for hi, rs in enumerate(halves):
            for ch in range(nch):
                n = hi * nch + ch
                rows = slice(ch * T, (ch + 1) * T)
                gated_ref[rs.start + ch * T:rs.start + (ch + 1) * T, lanes] = (
                    u[hi][rows, lanes] * sv[:, n * D:(n + 1) * D]).astype(BF16)
    for x_, rs in zip(x, halves):
        o_ref[rs, :] = x_ + jnp.dot(gated_ref[rs, :], wo_ref[...], preferred_element_type=F32)


def _gmlp(x2d, g, w_uv, v_norm, w_s, b_s_t, w_o):
    t, c = x2d.shape
    width = w_o.shape[0]
    tm = 512
    consts = [g, w_uv, v_norm, w_s, b_s_t, w_o]
    return pl.pallas_call(
        _gmlp_kernel,
        grid=(t // tm,),
        in_specs=[pl.BlockSpec((tm, c), lambda i: (i, 0))] + [_resident(a.shape) for a in consts],
        out_specs=pl.BlockSpec((tm, c), lambda i: (i, 0)),
        out_shape=jax.ShapeDtypeStruct((t, c), F32),
        scratch_shapes=[pltpu.VMEM((tm, width), BF16)],
        compiler_params=_params("parallel"),
        name="gmlp",
    )(x2d, *consts)


def _pad_cols(w, n):
    return jnp.pad(w, ((0, 0), (0, n - w.shape[1])))


def _pad_rows(w, n):
    return jnp.pad(w, ((0, n - w.shape[0]), (0, 0)))


def _round_up(n, m):
    return -(-n // m) * m


def _rwkv_layer(x3d, mix_g, mu, w_rkv, w0, w1, w2, a0, a1, a2, g1, g2, k_k, k_a, r_k,
                ln_w, ln_b, w_o):
    b, s, c = x3d.shape
    row = lambda p: p.reshape(1, c)
    rd = _round_up(w1.shape[1], LANES)
    ra = _round_up(a1.shape[1], LANES)
    rg = _round_up(g1.shape[1], LANES)
    r, lw, k, v, kk, bb, g = _rwkv_proj(
        x3d, row(mix_g), mu, w_rkv.astype(BF16), row(w0),
        _pad_cols(w1, rd).astype(BF16), _pad_rows(w2, rd).astype(BF16), row(a0),
        _pad_cols(a1, ra).astype(BF16), _pad_rows(a2, ra).astype(BF16),
        _pad_cols(g1, rg).astype(BF16), _pad_rows(g2, rg).astype(BF16),
        row(k_k), row(k_a))
    y = _rwkv_scan(r, lw, k, v, kk, bb)
    flat = lambda t: t.reshape(b * s, c)
    out = _rwkv_out(flat(x3d), flat(y), flat(r), flat(k), flat(v), flat(g),
                    row(r_k), row(ln_w), row(ln_b), w_o.astype(BF16))
    return out.reshape(b, s, c)


def kernel(x, mem, mem_norm, mem_w_kv, ffn1_norm, ffn1_w_in, ffn1_w_out, mix_norm, xattn_norm, xattn_wq, xattn_wo, ffn2_norm, ffn2_w_in, ffn2_w_out, rwkv_mu, rwkv_w_rkv, rwkv_w0, rwkv_w1, rwkv_w2, rwkv_a0, rwkv_a1, rwkv_a2, rwkv_g1, rwkv_g2, rwkv_k_k, rwkv_k_a, rwkv_r_k, rwkv_ln_w, rwkv_ln_b, rwkv_w_o, gmlp_w_uv, gmlp_v_norm, gmlp_w_s, gmlp_b_s, gmlp_w_o, final_norm):
    b, s, c = x.shape
    depth = ffn1_norm.shape[0]
    row = lambda p: p.reshape(1, -1)

    folded = _mem_fold(mem, row(mem_norm), mem_w_kv.astype(BF16), xattn_wq.astype(BF16),
                       xattn_wo.astype(BF16))

    w_in1, w_out1 = ffn1_w_in.astype(BF16), ffn1_w_out.astype(BF16)
    w_in2, w_out2 = ffn2_w_in.astype(BF16), ffn2_w_out.astype(BF16)
    x2d = x.reshape(b * s, c)
    for i in range(depth):
        x2d = _ffn(x2d, row(ffn1_norm[i]), w_in1, w_out1, i)
        j = i // 2
        if i % 2 == 0:
            x3d = _rwkv_layer(
                x2d.reshape(b, s, c), mix_norm[i], rwkv_mu[j], rwkv_w_rkv[j], rwkv_w0[j],
                rwkv_w1[j], rwkv_w2[j], rwkv_a0[j], rwkv_a1[j], rwkv_a2[j], rwkv_g1[j],
                rwkv_g2[j], rwkv_k_k[j], rwkv_k_a[j], rwkv_r_k[j], rwkv_ln_w[j],
                rwkv_ln_b[j], rwkv_w_o[j])
        else:
            x2d = _gmlp(x2d, row(mix_norm[i]), gmlp_w_uv[j].astype(BF16), row(gmlp_v_norm[j]),
                        gmlp_w_s[j], jnp.transpose(gmlp_b_s[j]), gmlp_w_o[j].astype(BF16))
            x3d = x2d.reshape(b, s, c)
        x3d = _xattn(x3d, row(xattn_norm[i]), *folded[i])
        x2d = _ffn(x3d.reshape(b * s, c), row(ffn2_norm[i]), w_in2, w_out2, i,
                   final_g=row(final_norm) if i == depth - 1 else None)
    return x2d.reshape(b, s, c)
```
